```python
import math
import jax, jax.numpy as jnp
from jax import lax
import numpy as np

D_MODEL = 1024
BATCH = 2
SEQ = 8192
DEPTH = 1
DEC_BATCH = 16
DEC_SEQ = 32
PAST_LEN = 4096

CHUNK = 64
D_PLE = 256
D_RNN = 1024
RG_BLOCKS = 8
RG_BLOCK = D_RNN // RG_BLOCKS
CONV_W = 4
RG_C = 8.0
N_HEADS = 8
HEAD_DIM = 128
N_KV_HEADS = 2
IDX_HEADS = 16
IDX_DIM = 64
TOPK_MAX = 256
Q_BLOCK = 128
NUM_BUCKETS = 32
MAX_DISTANCE = 128
PEER_HEADS = 8
PEER_NKEYS = 128
PEER_EXPERTS = PEER_NKEYS * PEER_NKEYS
PEER_DK = 256
PEER_TOPK = 16
PEER_BLOCK = 128
EPS = 1e-6
NEG = -1e30

IN_SPLITS = (D_RNN, D_RNN, N_HEADS * HEAD_DIM, N_KV_HEADS * HEAD_DIM, N_KV_HEADS * HEAD_DIM,
             IDX_HEADS * IDX_DIM, IDX_DIM, IDX_HEADS, D_MODEL, D_MODEL)
D_IN = sum(IN_SPLITS)

kernel_name = "hybrid_rglru_dsa_peer_stream_step"


def rmsnorm(x, g):
    xf = x.astype(jnp.float32)
    y = xf * lax.rsqrt(jnp.mean(xf * xf, axis=-1, keepdims=True) + EPS)
    return (y * g.astype(jnp.float32)).astype(x.dtype)


def causal_conv(x, state, w, b):
    T = x.shape[1]
    xp = jnp.concatenate([state.astype(x.dtype), x], axis=1)
    y = b
    for j in range(CONV_W):
        y = y + w[j] * xp[:, j:j + T]
    return y, xp[:, -(CONV_W - 1):]


def rglru(xc, h0, w_a, b_a, w_x, b_x, lam):
    B, T, _ = xc.shape
    xb = xc.reshape(B, T, RG_BLOCKS, RG_BLOCK)
    r = jax.nn.sigmoid(jnp.einsum('btnc,ncd->btnd', xb, w_a).reshape(B, T, D_RNN) + b_a)
    i = jax.nn.sigmoid(jnp.einsum('btnc,ncd->btnd', xb, w_x).reshape(B, T, D_RNN) + b_x)
    log_a = -RG_C * r.astype(jnp.float32) * jax.nn.softplus(-lam.astype(jnp.float32))
    a = jnp.exp(log_a)
    bb = jnp.sqrt(-jnp.expm1(2.0 * log_a)) * (i * xc).astype(jnp.float32)
    bb = bb.at[:, 0].add(a[:, 0] * h0.astype(jnp.float32))

    def comb(lhs, rhs):
        al, bl = lhs
        ar, br = rhs
        return al * ar, ar * bl + br

    _, h = lax.associative_scan(comb, (a, bb), axis=1)
    return h.astype(xc.dtype), h[:, -1].astype(xc.dtype)


def t5_bucket(rel):
    nb = NUM_BUCKETS // 2
    max_exact = nb // 2
    ret = jnp.where(rel > 0, nb, 0)
    n = jnp.abs(rel)
    n_f = jnp.maximum(n, 1).astype(jnp.float32)
    large = max_exact + (jnp.log(n_f / max_exact) / math.log(MAX_DISTANCE / max_exact)
                         * (nb - max_exact)).astype(jnp.int32)
    large = jnp.minimum(large, nb - 1)
    return ret + jnp.where(n < max_exact, n, large)


def dsa_attend(q, qi, wi, q_pos, k, v, ki, k_pos, rel_bias, n_sel):
    B, Tq = q.shape[:2]
    s = jnp.einsum('bthd,bsd->bths', qi, ki).astype(jnp.float32) * (IDX_DIM ** -0.5)
    score = jnp.einsum('bths,bth->bts', jax.nn.relu(s),
                       wi.astype(jnp.float32) * (IDX_HEADS ** -0.5))
    q_chunk = q_pos // CHUNK
    adm = (k_pos[None, :] // CHUNK) <= q_chunk[:, None]
    score = jnp.where(adm[None], score, -jnp.inf)
    _, sel = lax.top_k(score, n_sel)
    kg = jax.vmap(lambda arr, idx: arr[idx])(k, sel)
    vg = jax.vmap(lambda arr, idx: arr[idx])(v, sel)
    sel_pos = k_pos[sel]
    valid = (sel_pos // CHUNK) <= q_chunk[None, :, None]
    bias = rel_bias[t5_bucket(sel_pos - q_pos[None, :, None])]
    G = N_HEADS // N_KV_HEADS
    bias = jnp.moveaxis(bias, -1, 2).reshape(B, Tq, N_KV_HEADS, G, n_sel)
    qg = q.reshape(B, Tq, N_KV_HEADS, G, HEAD_DIM)
    logits = jnp.einsum('btkgd,btskd->btkgs', qg, kg).astype(jnp.float32) * (HEAD_DIM ** -0.5)
    logits = jnp.where(valid[:, :, None, None, :], logits + bias.astype(jnp.float32), NEG)
    p = jax.nn.softmax(logits, axis=-1).astype(v.dtype)
    o = jnp.einsum('btkgs,btskd->btkgd', p, vg)
    return o.reshape(B, Tq, N_HEADS * HEAD_DIM)


def peer_block(x, w_q, sub_keys, u, v):
    T = x.shape[0]
    q = (x @ w_q).reshape(T, PEER_HEADS, 2, PEER_DK // 2)
    s = jnp.einsum('thcd,hcnd->thcn', q, sub_keys).astype(jnp.float32)
    s1, i1 = lax.top_k(s[:, :, 0], PEER_TOPK)
    s2, i2 = lax.top_k(s[:, :, 1], PEER_TOPK)
    cand = (s1[..., :, None] + s2[..., None, :]).reshape(T, PEER_HEADS, PEER_TOPK * PEER_TOPK)
    cidx = (i1[..., :, None] * PEER_NKEYS + i2[..., None, :]).reshape(T, PEER_HEADS, PEER_TOPK * PEER_TOPK)
    top, pos = lax.top_k(cand, PEER_TOPK)
    e = jnp.take_along_axis(cidx, pos, axis=-1)
    g = jax.nn.softmax(top, axis=-1)
    ue = u[e]
    ve = v[e]
    act = jax.nn.gelu(jnp.einsum('thkd,td->thk', ue, x).astype(jnp.float32))
    return jnp.einsum('thk,thkd->td', (g * act).astype(x.dtype), ve)


def peer_ffn(x, w_q, sub_keys, u, v):
    B, T, D = x.shape
    n = B * T
    nb = -(-n // PEER_BLOCK)
    xf = jnp.pad(x.reshape(n, D), ((0, nb * PEER_BLOCK - n), (0, 0)))
    out = lax.map(lambda xb: peer_block(xb, w_q, sub_keys, u, v), xf.reshape(nb, PEER_BLOCK, D))
    return out.reshape(nb * PEER_BLOCK, D)[:n].reshape(B, T, D)


def trunk_layer(x, pl, conv_state, h_state, cache_k, cache_v, cache_ki, rel_bias,
                g_mix, w_in, conv_w, conv_b, w_rg_a, b_rg_a, w_rg_x, b_rg_x, rg_lambda,
                w_a_out, w_b_out, w_o, g_ffn, w_peer_q, peer_sub_keys, peer_u, peer_v,
                g_ple, w_ple_gate, w_ple_proj):
    B, T, _ = x.shape
    n = rmsnorm(x, g_mix)
    z = n @ w_in
    cuts = [int(c) for c in np.cumsum(IN_SPLITS)[:-1]]
    rnn_x, rnn_gate, q, k, v, qi, ki, wi, gate_a, gate_b = jnp.split(z, cuts, axis=-1)

    xc, conv_new = causal_conv(rnn_x, conv_state, conv_w, conv_b)
    hseq, h_last = rglru(xc, h_state, w_rg_a, b_rg_a, w_rg_x, b_rg_x, rg_lambda)
    y_a = (hseq * jax.nn.gelu(rnn_gate)) @ w_a_out

    q = q.reshape(B, T, N_HEADS, HEAD_DIM)
    k = k.reshape(B, T, N_KV_HEADS, HEAD_DIM)
    v = v.reshape(B, T, N_KV_HEADS, HEAD_DIM)
    qi = qi.reshape(B, T, IDX_HEADS, IDX_DIM)
    if cache_k is None:
        past = 0
        k_all, v_all, ki_all = k, v, ki
    else:
        past = cache_k.shape[1]
        k_all = jnp.concatenate([cache_k.astype(k.dtype), k], axis=1)
        v_all = jnp.concatenate([cache_v.astype(v.dtype), v], axis=1)
        ki_all = jnp.concatenate([cache_ki.astype(ki.dtype), ki], axis=1)
    L = past + T
    n_sel = min(TOPK_MAX, L // 4)
    k_pos = jnp.arange(L, dtype=jnp.int32)
    q_pos = past + jnp.arange(T, dtype=jnp.int32)
    if T % Q_BLOCK == 0:
        nbq = T // Q_BLOCK
        blk = lambda a: jnp.swapaxes(a.reshape((B, nbq, Q_BLOCK) + a.shape[2:]), 0, 1)
        out = lax.map(
            lambda args: dsa_attend(args[0], args[1], args[2], args[3], k_all, v_all, ki_all,
                                    k_pos, rel_bias, n_sel),
            (blk(q), blk(qi), blk(wi), q_pos.reshape(nbq, Q_BLOCK)))
        attn = jnp.swapaxes(out, 0, 1).reshape(B, T, N_HEADS * HEAD_DIM)
    else:
        attn = dsa_attend(q, qi, wi, q_pos, k_all, v_all, ki_all, k_pos, rel_bias, n_sel)
    y_b = attn @ w_b_out

    m = jax.nn.sigmoid(gate_a) * y_a + jax.nn.sigmoid(gate_b) * y_b
    x = x + m @ w_o
    x = x + peer_ffn(rmsnorm(x, g_ffn), w_peer_q, peer_sub_keys, peer_u, peer_v)
    x = x + jax.nn.sigmoid(rmsnorm(x, g_ple) @ w_ple_gate) * (pl @ w_ple_proj)
    return x, (k, v, ki, conv_new, h_last)


def setup_inputs(seed: int = 0) -> dict:
    key = jax.random.key(seed)
    ks = iter(jax.random.split(key, 48))

    def nrm(shape, scale):
        return jax.random.normal(next(ks), shape, jnp.float32) * scale

    D = D_MODEL
    a0 = jax.random.uniform(next(ks), (DEPTH, D_RNN), jnp.float32, minval=0.9, maxval=0.999)
    a_base = a0 ** (1.0 / RG_C)
    rg_lambda = jnp.log(a_base) - jnp.log1p(-a_base)
    return {
        "x_prompt": nrm((BATCH, SEQ, D), 1.0),
        "x_sample": nrm((DEC_BATCH, DEC_SEQ, D), 1.0),
        "p_prompt": nrm((DEPTH, BATCH, SEQ, D_PLE), 1.0),
        "p_sample": nrm((DEPTH, DEC_BATCH, DEC_SEQ, D_PLE), 1.0),
        "state_conv": nrm((DEPTH, DEC_BATCH, CONV_W - 1, D_RNN), 1.0),
        "state_rglru": nrm((DEPTH, DEC_BATCH, D_RNN), 0.5),
        "cache_k": nrm((DEPTH, DEC_BATCH, PAST_LEN, N_KV_HEADS, HEAD_DIM), 1.0),
        "cache_v": nrm((DEPTH, DEC_BATCH, PAST_LEN, N_KV_HEADS, HEAD_DIM), 1.0),
        "cache_idx_k": nrm((DEPTH, DEC_BATCH, PAST_LEN, IDX_DIM), 1.0),
        "rel_bias": nrm((NUM_BUCKETS, N_HEADS), 0.5),
        "g_mix": 1.0 + nrm((DEPTH, D), 0.05),
        "w_in": nrm((DEPTH, D, D_IN), D ** -0.5),
        "conv_w": nrm((DEPTH, CONV_W, D_RNN), CONV_W ** -0.5),
        "conv_b": nrm((DEPTH, D_RNN), 0.02),
        "w_rg_a": nrm((DEPTH, RG_BLOCKS, RG_BLOCK, RG_BLOCK), RG_BLOCK ** -0.5),
        "b_rg_a": nrm((DEPTH, D_RNN), 0.02),
        "w_rg_x": nrm((DEPTH, RG_BLOCKS, RG_BLOCK, RG_BLOCK), RG_BLOCK ** -0.5),
        "b_rg_x": nrm((DEPTH, D_RNN), 0.02),
        "rg_lambda": rg_lambda,
        "w_a_out": nrm((DEPTH, D_RNN, D), D_RNN ** -0.5),
        "w_b_out": nrm((DEPTH, N_HEADS * HEAD_DIM, D), (N_HEADS * HEAD_DIM) ** -0.5),
        "w_o": nrm((DEPTH, D, D), D ** -0.5),
        "g_ffn": 1.0 + nrm((DEPTH, D), 0.05),
        "w_peer_q": nrm((DEPTH, D, PEER_HEADS * PEER_DK), D ** -0.5),
        "peer_sub_keys": nrm((DEPTH, PEER_HEADS, 2, PEER_NKEYS, PEER_DK // 2), (PEER_DK // 2) ** -0.5),
        "peer_u": nrm((DEPTH, PEER_EXPERTS, D), D ** -0.5),
        "peer_v": nrm((DEPTH, PEER_EXPERTS, D), 0.3),
        "g_ple": 1.0 + nrm((DEPTH, D), 0.05),
        "w_ple_gate": nrm((DEPTH, D, D), D ** -0.5),
        "w_ple_proj": nrm((DEPTH, D_PLE, D), D_PLE ** -0.5),
        "g_final": 1.0 + nrm((D,), 0.05),
    }


def reference(x_prompt, x_sample, p_prompt, p_sample, state_conv, state_rglru, cache_k, cache_v,
              cache_idx_k, rel_bias, g_mix, w_in, conv_w, conv_b, w_rg_a, b_rg_a, w_rg_x, b_rg_x,
              rg_lambda, w_a_out, w_b_out, w_o, g_ffn, w_peer_q, peer_sub_keys, peer_u, peer_v,
              g_ple, w_ple_gate, w_ple_proj, g_final):
    hp, hs = x_prompt, x_sample
    B = x_prompt.shape[0]
    kp, vp, kip, cp, rp = [], [], [], [], []
    ksl, vsl, kisl, csl, rsl = [], [], [], [], []
    for i in range(DEPTH):
        wts = (rel_bias, g_mix[i], w_in[i], conv_w[i], conv_b[i], w_rg_a[i], b_rg_a[i], w_rg_x[i],
               b_rg_x[i], rg_lambda[i], w_a_out[i], w_b_out[i], w_o[i], g_ffn[i], w_peer_q[i],
               peer_sub_keys[i], peer_u[i], peer_v[i], g_ple[i], w_ple_gate[i], w_ple_proj[i])
        zc = jnp.zeros((B, CONV_W - 1, D_RNN), x_prompt.dtype)
        zh = jnp.zeros((B, D_RNN), x_prompt.dtype)
        hp, (k1, v1, ki1, c1, r1) = trunk_layer(hp, p_prompt[i], zc, zh, None, None, None, *wts)
        hs, (k2, v2, ki2, c2, r2) = trunk_layer(hs, p_sample[i], state_conv[i], state_rglru[i],
                                                cache_k[i], cache_v[i], cache_idx_k[i], *wts)
        kp.append(k1); vp.append(v1); kip.append(ki1); cp.append(c1); rp.append(r1)
        ksl.append(k2); vsl.append(v2); kisl.append(ki2); csl.append(c2); rsl.append(r2)
    y_prompt = rmsnorm(hp, g_final)
    y_sample = rmsnorm(hs, g_final)
    return (y_prompt, y_sample,
            jnp.stack(kp), jnp.stack(vp), jnp.stack(kip), jnp.stack(cp), jnp.stack(rp),
            jnp.stack(ksl), jnp.stack(vsl), jnp.stack(kisl), jnp.stack(csl), jnp.stack(rsl))
```

```python
import functools
import math

import jax
import jax.numpy as jnp
from jax import lax
from jax.experimental import pallas as pl
from jax.experimental.pallas import tpu as pltpu

F32 = jnp.float32
BF16 = jnp.bfloat16
I32 = jnp.int32

D_MODEL = 1024
CHUNK = 64
CHUNK_SHIFT = 6
D_PLE = 256
D_RNN = 1024
RG_BLOCKS = 8
RG_BLOCK = D_RNN // RG_BLOCKS
CONV_W = 4
RG_C = 8.0
N_HEADS = 8
HEAD_DIM = 128
N_KV_HEADS = 2
KV_GROUP = N_HEADS // N_KV_HEADS
IDX_HEADS = 16
IDX_DIM = 64
TOPK_MAX = 256
NUM_BUCKETS = 32
MAX_DISTANCE = 128
PEER_HEADS = 8
PEER_NKEYS = 128
PEER_EXPERTS = PEER_NKEYS * PEER_NKEYS
PEER_DK = 256
PEER_TOPK = 16
EPS = 1e-6
NEG = -1e30
INT_MIN = -(2 ** 31)

LANES = 128
VMEM_LIMIT = 56 * 1024 * 1024

_C_RX, _C_RG, _C_Q, _C_K, _C_V, _C_QI, _C_GA, _C_GB, _C_TAIL, _C_END = (
    0, 1024, 2048, 3072, 3328, 3584, 4608, 5632, 6656, 6784)

TM = 256
DSA_BLK = 256
PEER_TM = 512
PEER_EB = 512


def _params(sem):
    return pltpu.CompilerParams(dimension_semantics=sem, vmem_limit_bytes=VMEM_LIMIT)


def _const_spec(shape):
    nd = len(shape)
    return pl.BlockSpec(shape, lambda *_: (0,) * nd, pipeline_mode=pl.Buffered(1))


def _rms(x, g):
    return x * lax.rsqrt(jnp.mean(x * x, axis=-1, keepdims=True) + EPS) * g


def _gelu(x):
    return 0.5 * x * (1.0 + jnp.tanh(math.sqrt(2.0 / math.pi) * (x + 0.044715 * (x * x * x))))


def _sigmoid(x):
    return 1.0 / (1.0 + jnp.exp(-x))


def _dot(a, b):
    return jnp.dot(a, b, preferred_element_type=F32)


def _dot_nt(a, b):
    return lax.dot_general(a, b, (((1,), (1,)), ((), ())), preferred_element_type=F32)


def _inproj_kernel(x_ref, g_ref, w_ref, rx_ref, rg_ref, q_ref, k_ref, v_ref, kb_ref, vb_ref,
                   qi_ref, ga_ref, gb_ref, ki_ref, kib_ref, wi_ref):
    n = _rms(x_ref[...], g_ref[...]).astype(BF16)

    def mm(a, b):
        return _dot(n, w_ref[:, a:b])

    rx_ref[...] = mm(_C_RX, _C_RG)
    rg_ref[...] = mm(_C_RG, _C_Q)
    q_ref[...] = mm(_C_Q, _C_K).astype(BF16)
    k = mm(_C_K, _C_V)
    k_ref[...] = k
    kb_ref[...] = k.astype(BF16)
    v = mm(_C_V, _C_QI)
    v_ref[...] = v
    vb_ref[...] = v.astype(BF16)
    qi_ref[...] = mm(_C_QI, _C_GA).astype(BF16)
    ga_ref[...] = mm(_C_GA, _C_GB)
    gb_ref[...] = mm(_C_GB, _C_TAIL)
    tail = mm(_C_TAIL, _C_END)
    ki = tail[:, :IDX_DIM]
    ki_ref[...] = ki
    kib_ref[...] = ki.astype(BF16)
    wi_ref[...] = tail[:, IDX_DIM:IDX_DIM + IDX_HEADS]


def _inproj(x2, g_mix, w_r):
    n = x2.shape[0]
    kvd = N_KV_HEADS * HEAD_DIM
    tok = lambda w: pl.BlockSpec((TM, w), lambda i: (i, 0))
    widths_dtypes = [(D_RNN, F32), (D_RNN, F32), (D_MODEL, BF16), (kvd, F32), (kvd, F32), (kvd, BF16),
                     (kvd, BF16), (IDX_HEADS * IDX_DIM, BF16), (D_MODEL, F32), (D_MODEL, F32),
                     (IDX_DIM, F32), (IDX_DIM, BF16), (IDX_HEADS, F32)]
    return pl.pallas_call(
        _inproj_kernel,
        grid=(n // TM,),
        in_specs=[tok(D_MODEL), _const_spec((1, D_MODEL)), _const_spec(w_r.shape)],
        out_specs=[tok(w) for w, _ in widths_dtypes],
        out_shape=[jax.ShapeDtypeStruct((n, w), dt) for w, dt in widths_dtypes],
        compiler_params=_params(("parallel",)),
        name="inproj",
    )(x2, g_mix, w_r)


def _rglru_kernel(x_ref, gate_ref, cs_ref, h0_ref, cw_ref, cb_ref, wa_ref, ba_ref, wx_ref, bx_ref,
                  lam_ref, hg_ref, cnew_ref, hlast_ref, xp_s, hc_s, *, tt):
    @pl.when(pl.program_id(1) == 0)
    def _():
        xp_s[5:8, :] = cs_ref[0]
        hc_s[...] = h0_ref[0]

    xp_s[8:8 + tt, :] = x_ref[0]
    nl = -lam_ref[...]
    softplus = jnp.maximum(nl, 0.0) + jnp.log1p(jnp.exp(-jnp.abs(nl)))
    row = lax.broadcasted_iota(I32, (tt, RG_BLOCK), 0)
    for n in range(RG_BLOCKS):
        sl = slice(n * RG_BLOCK, (n + 1) * RG_BLOCK)
        xc = cb_ref[:, sl]
        for j in range(CONV_W):
            xc = xc + cw_ref[j:j + 1, sl] * xp_s[5 + j:5 + j + tt, sl]
        xcb = xc.astype(BF16)
        r = _sigmoid(_dot(xcb, wa_ref[n]) + ba_ref[:, sl])
        ig = _sigmoid(_dot(xcb, wx_ref[n]) + bx_ref[:, sl])
        log_a = -RG_C * r * softplus[:, sl]
        a = jnp.exp(log_a)
        mult = jnp.sqrt(-jnp.tanh(log_a) * (a * a + 1.0))
        bv = mult * (ig * xc)
        d = 1
        while d < tt:
            keep = row >= d
            a_sh = pltpu.roll(a, d, 0)
            b_sh = pltpu.roll(bv, d, 0)
            bv = jnp.where(keep, a * b_sh + bv, bv)
            a = jnp.where(keep, a * a_sh, a)
            d *= 2
        h = a * hc_s[:, sl] + bv
        hc_s[:, sl] = h[tt - 1:tt, :]
        hg_ref[0, :, sl] = (h * _gelu(gate_ref[0, :, sl])).astype(BF16)
    tail = xp_s[tt + 5:tt + 8, :]
    cnew_ref[0] = tail
    xp_s[5:8, :] = tail
    hlast_ref[0] = hc_s[...]


def _rglru(rx, rgate, conv_state, h0, conv_w, conv_b, w_a, b_a, w_x, b_x, lam):
    b, t, _ = rx.shape
    tt = min(t, 256)
    seq = pl.BlockSpec((1, tt, D_RNN), lambda i, j: (i, j, 0))
    per_b = lambda r: pl.BlockSpec((1, r, D_RNN), lambda i, j: (i, 0, 0))
    vec = _const_spec((1, D_RNN))
    wblk = _const_spec((RG_BLOCKS, RG_BLOCK, RG_BLOCK))
    return pl.pallas_call(
        functools.partial(_rglru_kernel, tt=tt),
        grid=(b, t // tt),
        in_specs=[seq, seq, per_b(CONV_W - 1), per_b(1), _const_spec((CONV_W, D_RNN)), vec,
                  wblk, vec, wblk, vec, vec],
        out_specs=[seq, per_b(CONV_W - 1), per_b(1)],
        out_shape=[jax.ShapeDtypeStruct((b, t, D_RNN), BF16),
                   jax.ShapeDtypeStruct((b, CONV_W - 1, D_RNN), F32),
                   jax.ShapeDtypeStruct((b, 1, D_RNN), F32)],
        scratch_shapes=[pltpu.VMEM((tt + 8, D_RNN), F32), pltpu.VMEM((1, D_RNN), F32)],
        compiler_params=_params(("parallel", "arbitrary")),
        name="rglru",
    )(rx, rgate, conv_state, h0, conv_w, conv_b, w_a, b_a, w_x, b_x, lam)


def _dsa_kernel(q_ref, qi_ref, wi_ref, kt_ref, v_ref, kie_ref, kio_ref, nb_ref, o_ref,
                key_s, wb_s, qs_s, m_s, l_s, acc_s, *, tq, past, l_real, n_sel):
    kb = DSA_BLK
    nc = kb // LANES
    i = pl.program_id(1)
    q0 = past + i * tq
    own = q0 // kb
    nkb = own + 1

    wi = wi_ref[0] * (IDX_HEADS ** -0.5 * IDX_DIM ** -0.5)
    for h in range(IDX_HEADS):
        wb_s[h] = jnp.broadcast_to(wi[:, h:h + 1], (tq, LANES))
    q_chunk = (q0 + lax.broadcasted_iota(I32, (tq, LANES), 0)) >> CHUNK_SHIFT
    lane = lax.broadcasted_iota(I32, (tq, LANES), 1)

    def score_block(j, carry):
        k0 = pl.multiple_of(j * kb, kb)
        acc = [jnp.zeros((tq, LANES), F32) for _ in range(nc)]
        for h2 in range(IDX_HEADS // 2):
            qpair = qi_ref[0, :, h2 * LANES:(h2 + 1) * LANES]
            for par, kref in ((0, kie_ref), (1, kio_ref)):
                s = _dot(qpair, kref[0, :, pl.ds(k0, kb)])
                w = wb_s[2 * h2 + par]
                for c in range(nc):
                    acc[c] = acc[c] + jnp.maximum(s[:, c * LANES:(c + 1) * LANES], 0.0) * w
        for c in range(nc):
            bits = pltpu.bitcast(acc[c], I32)
            key = bits ^ ((bits >> 31) & 0x7FFFFFFF)
            kpos = k0 + c * LANES + lane
            key = jnp.where((kpos >> CHUNK_SHIFT) <= q_chunk, key, INT_MIN)
            key = jnp.where(kpos < l_real, key, INT_MIN)
            key_s[:, pl.ds(pl.multiple_of(k0 + c * LANES, LANES), LANES)] = key
        return carry

    lax.fori_loop(0, nkb, score_block, 0)

    def count_ge(cand):
        def body(j, cnt):
            k0 = pl.multiple_of(j * kb, kb)
            for c in range(nc):
                blk = key_s[:, pl.ds(pl.multiple_of(k0 + c * LANES, LANES), LANES)]
                cnt = cnt + jnp.where(blk >= cand, 1, 0)
            return cnt
        cnt = lax.fori_loop(0, nkb, body, jnp.zeros((tq, LANES), I32))
        return jnp.broadcast_to(jnp.sum(cnt, axis=1, keepdims=True), (tq, LANES))

    zero = jnp.zeros((tq, LANES), I32)
    res = jnp.where(count_ge(zero) >= n_sel, zero, INT_MIN)

    def bit_step(it, res):
        cand = res | (1 << (30 - it))
        return jnp.where(count_ge(cand) >= n_sel, cand, res)

    res = lax.fori_loop(0, 31, bit_step, res)
    thr = jnp.maximum(res, INT_MIN + 1)

    for g in range(N_KV_HEADS):
        for hh in range(KV_GROUP):
            h = g * KV_GROUP + hh
            qs_s[g, hh * tq:(hh + 1) * tq, :] = q_ref[0, :, h * HEAD_DIM:(h + 1) * HEAD_DIM]
    m_s[...] = jnp.full(m_s.shape, NEG, F32)
    l_s[...] = jnp.zeros(l_s.shape, F32)
    acc_s[...] = jnp.zeros(acc_s.shape, F32)
    scale = HEAD_DIM ** -0.5

    def attend(j, near):
        k0 = pl.multiple_of(j * kb, kb)
        mb = []
        for c in range(nc):
            blk = key_s[:, pl.ds(pl.multiple_of(k0 + c * LANES, LANES), LANES)]
            mb.append(jnp.where(blk >= thr, 0.0, NEG))
        for g in range(N_KV_HEADS):
            s_all = _dot(qs_s[g], kt_ref[0, g * HEAD_DIM:(g + 1) * HEAD_DIM, pl.ds(k0, kb)]) * scale
            vb = v_ref[0, pl.ds(k0, kb), g * HEAD_DIM:(g + 1) * HEAD_DIM]
            for hh in range(KV_GROUP):
                h = g * KV_GROUP + hh
                s = s_all[hh * tq:(hh + 1) * tq, :]
                sc = []
                for c in range(nc):
                    x = s[:, c * LANES:(c + 1) * LANES] + mb[c]
                    if near is not None:
                        x = x + nb_ref[near, h, :, c * LANES:(c + 1) * LANES]
                    sc.append(x)
                mx = sc[0]
                for c in range(1, nc):
                    mx = jnp.maximum(mx, sc[c])
                m_old = m_s[h]
                m_new = jnp.maximum(m_old, jnp.broadcast_to(jnp.max(mx, axis=1, keepdims=True), (tq, LANES)))
                alpha = jnp.exp(m_old - m_new)
                p = [jnp.exp(x - m_new) for x in sc]
                ps = p[0]
                for c in range(1, nc):
                    ps = ps + p[c]
                l_s[h] = alpha * l_s[h] + jnp.broadcast_to(jnp.sum(ps, axis=1, keepdims=True), (tq, LANES))
                m_s[h] = m_new
                pb = jnp.concatenate(p, axis=1).astype(BF16)
                acc_s[h] = alpha * acc_s[h] + _dot(pb, vb)

    def far_block(j, carry):
        attend(j, None)
        return carry

    lax.fori_loop(0, jnp.maximum(own - 1, 0), far_block, 0)

    @pl.when(own >= 1)
    def _():
        attend(own - 1, 0)

    attend(own, 1)

    for h in range(N_HEADS):
        o_ref[0, :, h * HEAD_DIM:(h + 1) * HEAD_DIM] = (acc_s[h] / l_s[h]).astype(BF16)


def _t5_bucket(rel):
    nb = NUM_BUCKETS // 2
    max_exact = nb // 2
    ret = jnp.where(rel > 0, nb, 0)
    n = jnp.abs(rel)
    n_f = jnp.maximum(n, 1).astype(F32)
    large = max_exact + (jnp.log(n_f / max_exact) / math.log(MAX_DISTANCE / max_exact)
                         * (nb - max_exact)).astype(I32)
    large = jnp.minimum(large, nb - 1)
    return ret + jnp.where(n < max_exact, n, large)


def _near_bias(rel_bias, tq):
    a = jnp.arange(tq, dtype=I32)[:, None]
    b = jnp.arange(DSA_BLK, dtype=I32)[None, :]
    far = rel_bias[_t5_bucket(jnp.asarray(-MAX_DISTANCE, I32))]
    tiles = []
    for d in (-DSA_BLK, 0):
        rel = d + b - a
        t = rel_bias[_t5_bucket(rel)] - far
        t = jnp.where((rel > -MAX_DISTANCE)[..., None], t, 0.0)
        tiles.append(jnp.moveaxis(t, -1, 0))
    return jnp.stack(tiles)


def _dsa(q, qi, wi, kt, v, kie, kio, nbias, *, past, l_real, n_sel):
    b, t, _ = q.shape
    tq = min(t, DSA_BLK)
    lpad = kt.shape[2]
    tile = lambda w: pl.BlockSpec((1, tq, w), lambda i, j: (i, j, 0))
    per_b = lambda r, c: pl.BlockSpec((1, r, c), lambda i, j: (i, 0, 0), pipeline_mode=pl.Buffered(1))
    kvd = N_KV_HEADS * HEAD_DIM
    return pl.pallas_call(
        functools.partial(_dsa_kernel, tq=tq, past=past, l_real=l_real, n_sel=n_sel),
        grid=(b, t // tq),
        in_specs=[tile(D_MODEL), tile(IDX_HEADS * IDX_DIM), tile(IDX_HEADS),
                  per_b(kvd, lpad), per_b(lpad, kvd), per_b(LANES, lpad), per_b(LANES, lpad),
                  _const_spec(nbias.shape)],
        out_specs=tile(D_MODEL),
        out_shape=jax.ShapeDtypeStruct((b, t, D_MODEL), BF16),
        scratch_shapes=[pltpu.VMEM((tq, lpad), I32),
                        pltpu.VMEM((IDX_HEADS, tq, LANES), F32),
                        pltpu.VMEM((N_KV_HEADS, KV_GROUP * tq, HEAD_DIM), BF16),
                        pltpu.VMEM((N_HEADS, tq, LANES), F32),
                        pltpu.VMEM((N_HEADS, tq, LANES), F32),
                        pltpu.VMEM((N_HEADS, tq, HEAD_DIM), F32)],
        compiler_params=_params(("parallel", "arbitrary")),
        name="dsa",
    )(q, qi, wi, kt, v, kie, kio, nbias)


def _merge_kernel(x_ref, hg_ref, at_ref, ga_ref, gb_ref, wa_ref, wb_ref, wo_ref, gf_ref, wq_ref, sk_ref,
                  x1_ref, xn_ref, st_ref):
    ya = _dot(hg_ref[...], wa_ref[...])
    yb = _dot(at_ref[...], wb_ref[...])
    m = _sigmoid(ga_ref[...]) * ya + _sigmoid(gb_ref[...]) * yb
    x1 = x_ref[...] + _dot(m.astype(BF16), wo_ref[...])
    x1_ref[...] = x1
    xn = _rms(x1, gf_ref[...]).astype(BF16)
    xn_ref[...] = xn
    qp = _dot(xn, wq_ref[...])
    for j in range(2 * PEER_HEADS):
        qj = qp[:, j * LANES:(j + 1) * LANES].astype(BF16)
        st_ref[j] = _dot_nt(sk_ref[j], qj)


def _merge(x2, hg, attn, ga, gb, w_a_out, w_b_out, w_o, g_ffn, w_q, sk):
    n = x2.shape[0]
    tok = lambda w: pl.BlockSpec((TM, w), lambda i: (i, 0))
    sq = _const_spec((D_MODEL, D_MODEL))
    nsk = 2 * PEER_HEADS
    return pl.pallas_call(
        _merge_kernel,
        grid=(n // TM,),
        in_specs=[tok(D_MODEL)] * 5 + [sq, sq, sq, _const_spec((1, D_MODEL)),
                                       _const_spec(w_q.shape), _const_spec(sk.shape)],
        out_specs=[tok(D_MODEL), tok(D_MODEL), pl.BlockSpec((nsk, PEER_NKEYS, TM), lambda i: (0, 0, i))],
        out_shape=[jax.ShapeDtypeStruct((n, D_MODEL), F32), jax.ShapeDtypeStruct((n, D_MODEL), BF16),
                   jax.ShapeDtypeStruct((nsk, PEER_NKEYS, n), F32)],
        compiler_params=_params(("parallel",)),
        name="merge",
    )(x2, hg, attn, ga, gb, w_a_out, w_b_out, w_o, g_ffn, w_q, sk)


def _peer_select_kernel(st_ref, s1m_ref, s2m_ref, e1_ref, e2_ref, thr_ref, c_s):
    ninf = -jnp.inf

    def top(s):
        vals, cur = [], s
        for _ in range(PEER_TOPK):
            m = jnp.max(cur, axis=0, keepdims=True)
            vals.append(m)
            cur = jnp.where(cur == m, ninf, cur)
        return vals, jnp.where(cur == s, ninf, s)

    def head(h, carry):
        v1, s1m = top(st_ref[2 * h])
        v2, s2m = top(st_ref[2 * h + 1])
        v2t = jnp.concatenate(v2, axis=0)
        for i in range(PEER_TOPK):
            c_s[i * PEER_TOPK:(i + 1) * PEER_TOPK, :] = v1[i] + v2t
        cand = c_s[...]
        cur = cand
        for r in range(PEER_TOPK):
            thr = jnp.max(cur, axis=0, keepdims=True)
            if r + 1 < PEER_TOPK:
                cur = jnp.where(cur == thr, ninf, cur)
        z = jnp.sum(jnp.where(cand >= thr, jnp.exp(cand - (v1[0] + v2[0])), 0.0), axis=0, keepdims=True)
        s1m_ref[h] = s1m
        s2m_ref[h] = s2m
        e1_ref[h] = jnp.exp(s1m - v1[0])
        e2_ref[h] = jnp.exp(s2m - v2[0]) / z
        thr_ref[pl.ds(h, 1), :] = thr
        return carry

    lax.fori_loop(0, PEER_HEADS, head, 0)


def _peer_select(st):
    n = st.shape[2]
    big = pl.BlockSpec((PEER_HEADS, PEER_NKEYS, PEER_TM), lambda i: (0, 0, i))
    sds = jax.ShapeDtypeStruct((PEER_HEADS, PEER_NKEYS, n), F32)
    return pl.pallas_call(
        _peer_select_kernel,
        grid=(n // PEER_TM,),
        in_specs=[pl.BlockSpec((2 * PEER_HEADS, PEER_NKEYS, PEER_TM), lambda i: (0, 0, i))],
        out_specs=[big, big, big, big, pl.BlockSpec((PEER_HEADS, PEER_TM), lambda i: (0, i))],
        out_shape=[sds, sds, sds, sds, jax.ShapeDtypeStruct((PEER_HEADS, n), F32)],
        scratch_shapes=[pltpu.VMEM((PEER_TOPK * PEER_TOPK, PEER_TM), F32)],
        compiler_params=_params(("parallel",)),
        name="peer_select",
    )(st)


def _peer_dense_kernel(xnt_ref, u_ref, vt_ref, s1m_ref, s2m_ref, e1_ref, e2_ref, thr_ref, o_ref,
                       acc_s, g_s):
    e = pl.program_id(1)
    per_step = PEER_EB // PEER_NKEYS

    @pl.when(e == 0)
    def _():
        acc_s[...] = jnp.zeros(acc_s.shape, F32)

    act_t = _dot(u_ref[...], xnt_ref[...])
    for aa in range(per_step):
        a = e * per_step + aa
        w = jnp.zeros((PEER_NKEYS, PEER_TM), F32)
        for h in range(PEER_HEADS):
            c = s2m_ref[h] + s1m_ref[h, pl.ds(a, 1), :]
            w = w + jnp.where(c >= thr_ref[h:h + 1, :], e2_ref[h], 0.0) * e1_ref[h, pl.ds(a, 1), :]
        rows = slice(aa * PEER_NKEYS, (aa + 1) * PEER_NKEYS)
        g_s[rows, :] = (w * _gelu(act_t[rows, :])).astype(BF16)
    acc_s[...] += _dot(vt_ref[...], g_s[...])

    @pl.when(e == pl.num_programs(1) - 1)
    def _():
        o_ref[...] = acc_s[...].T


def _peer_dense(xnt, u_b, vt_b, s1m, s2m, e1, e2, thr):
    n = xnt.shape[1]
    big = pl.BlockSpec((PEER_HEADS, PEER_NKEYS, PEER_TM), lambda i, e: (0, 0, i))
    return pl.pallas_call(
        _peer_dense_kernel,
        grid=(n // PEER_TM, PEER_EXPERTS // PEER_EB),
        in_specs=[pl.BlockSpec((D_MODEL, PEER_TM), lambda i, e: (0, i)),
                  pl.BlockSpec((PEER_EB, D_MODEL), lambda i, e: (e, 0)),
                  pl.BlockSpec((D_MODEL, PEER_EB), lambda i, e: (0, e)),
                  big, big, big, big,
                  pl.BlockSpec((PEER_HEADS, PEER_TM), lambda i, e: (0, i))],
        out_specs=pl.BlockSpec((PEER_TM, D_MODEL), lambda i, e: (i, 0)),
        out_shape=jax.ShapeDtypeStruct((n, D_MODEL), F32),
        scratch_shapes=[pltpu.VMEM((D_MODEL, PEER_TM), F32), pltpu.VMEM((PEER_EB, PEER_TM), BF16)],
        compiler_params=_params(("parallel", "arbitrary")),
        name="peer_dense",
    )(xnt, u_b, vt_b, s1m, s2m, e1, e2, thr)


def _final_kernel(x1_ref, po_ref, p_ref, gp_ref, wg_ref, wp_ref, gfin_ref, y_ref):
    x2 = x1_ref[...] + po_ref[...]
    gate = _sigmoid(_dot(_rms(x2, gp_ref[...]).astype(BF16), wg_ref[...]))
    x3 = x2 + gate * _dot(p_ref[...].astype(BF16), wp_ref[...])
    y_ref[...] = _rms(x3, gfin_ref[...])


def _final(x1, po, p2, g_ple, w_gate, w_proj, g_final):
    n = x1.shape[0]
    tok = lambda w: pl.BlockSpec((TM, w), lambda i: (i, 0))
    vec = _const_spec((1, D_MODEL))
    return pl.pallas_call(
        _final_kernel,
        grid=(n // TM,),
        in_specs=[tok(D_MODEL), tok(D_MODEL), tok(D_PLE), vec, _const_spec((D_MODEL, D_MODEL)),
                  _const_spec((D_PLE, D_MODEL)), vec],
        out_specs=tok(D_MODEL),
        out_shape=jax.ShapeDtypeStruct((n, D_MODEL), F32),
        compiler_params=_params(("parallel",)),
        name="final",
    )(x1, po, p2, g_ple, w_gate, w_proj, g_final)


def _pad_keys(x, lpad, axis):
    pad = lpad - x.shape[axis]
    if pad == 0:
        return x
    widths = [(0, 0)] * x.ndim
    widths[axis] = (0, pad)
    return jnp.pad(x, widths)


def _layer(x, p, conv_state, h_state, cache_k, cache_v, cache_ki, w):
    b, t, _ = x.shape
    n = b * t
    kvd = N_KV_HEADS * HEAD_DIM
    (rx, rgate, q, k, v, kb, vb, qi, ga, gb, ki, kib, wi) = _inproj(x.reshape(n, D_MODEL), w["g_mix"], w["w_in"])

    hg, conv_new, h_last = _rglru(rx.reshape(b, t, D_RNN), rgate.reshape(b, t, D_RNN), conv_state,
                                  h_state.reshape(b, 1, D_RNN), w["conv_w"], w["conv_b"], w["w_rg_a"],
                                  w["b_rg_a"], w["w_rg_x"], w["b_rg_x"], w["rg_lambda"])

    k_all, v_all, ki_all = kb.reshape(b, t, kvd), vb.reshape(b, t, kvd), kib.reshape(b, t, IDX_DIM)
    past = 0
    if cache_k is not None:
        past = cache_k.shape[1]
        k_all = jnp.concatenate([cache_k.reshape(b, past, kvd).astype(BF16), k_all], axis=1)
        v_all = jnp.concatenate([cache_v.reshape(b, past, kvd).astype(BF16), v_all], axis=1)
        ki_all = jnp.concatenate([cache_ki.astype(BF16), ki_all], axis=1)
    l_real = past + t
    assert past % DSA_BLK == 0 and t % min(t, DSA_BLK) == 0 and n % PEER_TM == 0
    lpad = -(-l_real // DSA_BLK) * DSA_BLK
    kt = _pad_keys(jnp.swapaxes(k_all, 1, 2), lpad, 2)
    v_all = _pad_keys(v_all, lpad, 1)
    kit = _pad_keys(jnp.swapaxes(ki_all, 1, 2), lpad, 2)
    zeros = jnp.zeros_like(kit)
    kie = jnp.concatenate([kit, zeros], axis=1)
    kio = jnp.concatenate([zeros, kit], axis=1)
    tq = min(t, DSA_BLK)
    attn = _dsa(q.reshape(b, t, D_MODEL), qi.reshape(b, t, IDX_HEADS * IDX_DIM), wi.reshape(b, t, IDX_HEADS),
                kt, v_all, kie, kio, _near_bias(w["rel_bias"], tq),
                past=past, l_real=l_real, n_sel=min(TOPK_MAX, l_real // 4))

    x1, xn, st = _merge(x.reshape(n, D_MODEL), hg.reshape(n, D_RNN), attn.reshape(n, D_MODEL), ga, gb,
                        w["w_a_out"], w["w_b_out"], w["w_o"], w["g_ffn"], w["w_peer_q"], w["peer_sk"])
    s1m, s2m, e1, e2, thr = _peer_select(st)
    po = _peer_dense(xn.T, w["peer_u"], w["peer_vt"], s1m, s2m, e1, e2, thr)
    y = _final(x1, po, p.reshape(n, D_PLE), w["g_ple"], w["w_ple_gate"], w["w_ple_proj"], w["g_final"])
    return (y.reshape(b, t, D_MODEL), k.reshape(b, t, N_KV_HEADS, HEAD_DIM),
            v.reshape(b, t, N_KV_HEADS, HEAD_DIM), ki.reshape(b, t, IDX_DIM), conv_new,
            h_last.reshape(b, D_RNN))


def kernel(x_prompt, x_sample, p_prompt, p_sample, state_conv, state_rglru, cache_k, cache_v, cache_idx_k, rel_bias, g_mix, w_in, conv_w, conv_b, w_rg_a, b_rg_a, w_rg_x, b_rg_x, rg_lambda, w_a_out, w_b_out, w_o, g_ffn, w_peer_q, peer_sub_keys, peer_u, peer_v, g_ple, w_ple_gate, w_ple_proj, g_final):
    assert g_mix.shape[0] == 1, "single trunk layer"
    row = lambda a: a.reshape(1, -1)
    wi_full = w_in[0]
    cut = _C_GA
    tail = IDX_DIM + IDX_HEADS
    w_r = jnp.concatenate([wi_full[:, :cut], wi_full[:, cut + tail:], wi_full[:, cut:cut + tail],
                           jnp.zeros((D_MODEL, _C_END - _C_TAIL - tail), F32)], axis=1).astype(BF16)
    w = dict(
        g_mix=row(g_mix[0]), w_in=w_r, conv_w=conv_w[0], conv_b=row(conv_b[0]),
        w_rg_a=w_rg_a[0].astype(BF16), b_rg_a=row(b_rg_a[0]), w_rg_x=w_rg_x[0].astype(BF16),
        b_rg_x=row(b_rg_x[0]), rg_lambda=row(rg_lambda[0]), rel_bias=rel_bias,
        w_a_out=w_a_out[0].astype(BF16), w_b_out=w_b_out[0].astype(BF16), w_o=w_o[0].astype(BF16),
        g_ffn=row(g_ffn[0]), w_peer_q=w_peer_q[0].astype(BF16),
        peer_sk=peer_sub_keys[0].reshape(2 * PEER_HEADS, PEER_NKEYS, PEER_DK // 2).astype(BF16),
        peer_u=peer_u[0].astype(BF16), peer_vt=peer_v[0].astype(BF16).T,
        g_ple=row(g_ple[0]), w_ple_gate=w_ple_gate[0].astype(BF16), w_ple_proj=w_ple_proj[0].astype(BF16),
        g_final=row(g_final),
    )
    bp = x_prompt.shape[0]
    zc = jnp.zeros((bp, CONV_W - 1, D_RNN), F32)
    zh = jnp.zeros((bp, D_RNN), F32)
    yp, k1, v1, ki1, c1, r1 = _layer(x_prompt, p_prompt[0], zc, zh, None, None, None, w)
    ys, k2, v2, ki2, c2, r2 = _layer(x_sample, p_sample[0], state_conv[0], state_rglru[0],
                                     cache_k[0], cache_v[0], cache_idx_k[0], w)
    return (yp, ys, k1[None], v1[None], ki1[None], c1[None], r1[None],
            k2[None], v2[None], ki2[None], c2[None], r2[None])
```

```python
import functools
import math

import jax
import jax.numpy as jnp
from jax import lax
from jax.experimental import pallas as pl
from jax.experimental.pallas import tpu as pltpu

F32 = jnp.float32
BF16 = jnp.bfloat16
I32 = jnp.int32

D_MODEL = 1024
CHUNK = 64
CHUNK_SHIFT = 6
D_PLE = 256
D_RNN = 1024
RG_BLOCKS = 8
RG_BLOCK = D_RNN // RG_BLOCKS
CONV_W = 4
RG_C = 8.0
N_HEADS = 8
HEAD_DIM = 128
N_KV_HEADS = 2
KV_GROUP = N_HEADS // N_KV_HEADS
IDX_HEADS = 16
IDX_DIM = 64
TOPK_MAX = 256
NUM_BUCKETS = 32
MAX_DISTANCE = 128
PEER_HEADS = 8
PEER_NKEYS = 128
PEER_EXPERTS = PEER_NKEYS * PEER_NKEYS
PEER_DK = 256
PEER_TOPK = 16
EPS = 1e-6
NEG = -1e30
INT_MIN = -(2 ** 31)

LANES = 128
VMEM_LIMIT = 56 * 1024 * 1024

_C_RX, _C_RG, _C_Q, _C_K, _C_V, _C_QI, _C_GA, _C_GB, _C_TAIL, _C_END = (
    0, 1024, 2048, 3072, 3328, 3584, 4608, 5632, 6656, 6784)

TM = 256
DSA_BLK = 256
PEER_TM = 512
PEER_EB = 512
PEER_CHUNKS = 4


def _params(sem):
    return pltpu.CompilerParams(dimension_semantics=sem, vmem_limit_bytes=VMEM_LIMIT)


def _const_spec(shape):
    nd = len(shape)
    return pl.BlockSpec(shape, lambda *_: (0,) * nd, pipeline_mode=pl.Buffered(1))


def _rms(x, g):
    return x * lax.rsqrt(jnp.mean(x * x, axis=-1, keepdims=True) + EPS) * g


def _gelu(x):
    return 0.5 * x * (1.0 + jnp.tanh(math.sqrt(2.0 / math.pi) * (x + 0.044715 * (x * x * x))))


def _sigmoid(x):
    return 1.0 / (1.0 + jnp.exp(-x))


def _dot(a, b):
    return jnp.dot(a, b, preferred_element_type=F32)


def _dot_nt(a, b):
    return lax.dot_general(a, b, (((1,), (1,)), ((), ())), preferred_element_type=F32)


def _inproj_kernel(x_ref, g_ref, w_ref, rx_ref, rg_ref, q_ref, k_ref, v_ref, kb_ref, vb_ref,
                   qi_ref, ga_ref, gb_ref, ki_ref, kib_ref, wi_ref):
    n = _rms(x_ref[...], g_ref[...]).astype(BF16)

    def mm(a, b):
        return _dot(n, w_ref[:, a:b])

    rx_ref[...] = mm(_C_RX, _C_RG)
    rg_ref[...] = mm(_C_RG, _C_Q)
    q_ref[...] = mm(_C_Q, _C_K).astype(BF16)
    k = mm(_C_K, _C_V)
    k_ref[...] = k
    kb_ref[...] = k.astype(BF16)
    v = mm(_C_V, _C_QI)
    v_ref[...] = v
    vb_ref[...] = v.astype(BF16)
    qi_ref[...] = mm(_C_QI, _C_GA).astype(BF16)
    ga_ref[...] = mm(_C_GA, _C_GB)
    gb_ref[...] = mm(_C_GB, _C_TAIL)
    tail = mm(_C_TAIL, _C_END)
    ki = tail[:, :IDX_DIM]
    ki_ref[...] = ki
    kib_ref[...] = ki.astype(BF16)
    wi_ref[...] = tail[:, IDX_DIM:IDX_DIM + IDX_HEADS]


def _inproj(x2, g_mix, w_r):
    n = x2.shape[0]
    kvd = N_KV_HEADS * HEAD_DIM
    tok = lambda w: pl.BlockSpec((TM, w), lambda i: (i, 0))
    widths_dtypes = [(D_RNN, F32), (D_RNN, F32), (D_MODEL, BF16), (kvd, F32), (kvd, F32), (kvd, BF16),
                     (kvd, BF16), (IDX_HEADS * IDX_DIM, BF16), (D_MODEL, F32), (D_MODEL, F32),
                     (IDX_DIM, F32), (IDX_DIM, BF16), (IDX_HEADS, F32)]
    return pl.pallas_call(
        _inproj_kernel,
        grid=(n // TM,),
        in_specs=[tok(D_MODEL), _const_spec((1, D_MODEL)), _const_spec(w_r.shape)],
        out_specs=[tok(w) for w, _ in widths_dtypes],
        out_shape=[jax.ShapeDtypeStruct((n, w), dt) for w, dt in widths_dtypes],
        compiler_params=_params(("parallel",)),
        name="inproj",
    )(x2, g_mix, w_r)


def _rglru_kernel(x_ref, gate_ref, cs_ref, h0_ref, cw_ref, cb_ref, wa_ref, ba_ref, wx_ref, bx_ref,
                  lam_ref, hg_ref, cnew_ref, hlast_ref, xp_s, hc_s, *, tt):
    @pl.when(pl.program_id(1) == 0)
    def _():
        xp_s[5:8, :] = cs_ref[0]
        hc_s[...] = h0_ref[0]

    xp_s[8:8 + tt, :] = x_ref[0]
    nl = -lam_ref[...]
    softplus = jnp.maximum(nl, 0.0) + jnp.log1p(jnp.exp(-jnp.abs(nl)))
    row = lax.broadcasted_iota(I32, (tt, RG_BLOCK), 0)
    for n in range(RG_BLOCKS):
        sl = slice(n * RG_BLOCK, (n + 1) * RG_BLOCK)
        xc = cb_ref[:, sl]
        for j in range(CONV_W):
            xc = xc + cw_ref[j:j + 1, sl] * xp_s[5 + j:5 + j + tt, sl]
        xcb = xc.astype(BF16)
        r = _sigmoid(_dot(xcb, wa_ref[n]) + ba_ref[:, sl])
        ig = _sigmoid(_dot(xcb, wx_ref[n]) + bx_ref[:, sl])
        log_a = -RG_C * r * softplus[:, sl]
        a = jnp.exp(log_a)
        mult = jnp.sqrt(-jnp.tanh(log_a) * (a * a + 1.0))
        bv = mult * (ig * xc)
        d = 1
        while d < tt:
            keep = row >= d
            a_sh = pltpu.roll(a, d, 0)
            b_sh = pltpu.roll(bv, d, 0)
            bv = jnp.where(keep, a * b_sh + bv, bv)
            a = jnp.where(keep, a * a_sh, a)
            d *= 2
        h = a * hc_s[:, sl] + bv
        hc_s[:, sl] = h[tt - 1:tt, :]
        hg_ref[0, :, sl] = (h * _gelu(gate_ref[0, :, sl])).astype(BF16)
    tail = xp_s[tt + 5:tt + 8, :]
    cnew_ref[0] = tail
    xp_s[5:8, :] = tail
    hlast_ref[0] = hc_s[...]


def _rglru(rx, rgate, conv_state, h0, conv_w, conv_b, w_a, b_a, w_x, b_x, lam):
    b, t, _ = rx.shape
    tt = min(t, 256)
    seq = pl.BlockSpec((1, tt, D_RNN), lambda i, j: (i, j, 0))
    per_b = lambda r: pl.BlockSpec((1, r, D_RNN), lambda i, j: (i, 0, 0))
    vec = _const_spec((1, D_RNN))
    wblk = _const_spec((RG_BLOCKS, RG_BLOCK, RG_BLOCK))
    return pl.pallas_call(
        functools.partial(_rglru_kernel, tt=tt),
        grid=(b, t // tt),
        in_specs=[seq, seq, per_b(CONV_W - 1), per_b(1), _const_spec((CONV_W, D_RNN)), vec,
                  wblk, vec, wblk, vec, vec],
        out_specs=[seq, per_b(CONV_W - 1), per_b(1)],
        out_shape=[jax.ShapeDtypeStruct((b, t, D_RNN), BF16),
                   jax.ShapeDtypeStruct((b, CONV_W - 1, D_RNN), F32),
                   jax.ShapeDtypeStruct((b, 1, D_RNN), F32)],
        scratch_shapes=[pltpu.VMEM((tt + 8, D_RNN), F32), pltpu.VMEM((1, D_RNN), F32)],
        compiler_params=_params(("parallel", "arbitrary")),
        name="rglru",
    )(rx, rgate, conv_state, h0, conv_w, conv_b, w_a, b_a, w_x, b_x, lam)


def _dsa_kernel(q_ref, qi_ref, wi_ref, kt_ref, v_ref, kie_ref, kio_ref, nb_ref, o_ref,
                key_s, wb_s, qs_s, m_s, l_s, acc_s, *, tq, past, l_real, n_sel):
    kb = DSA_BLK
    nc = kb // LANES
    i = pl.program_id(1)
    q0 = past + i * tq
    own = q0 // kb
    nkb = own + 1

    wi = wi_ref[0] * (IDX_HEADS ** -0.5 * IDX_DIM ** -0.5)
    for h in range(IDX_HEADS):
        wb_s[h] = jnp.broadcast_to(wi[:, h:h + 1], (tq, LANES))
    q_chunk = (q0 + lax.broadcasted_iota(I32, (tq, LANES), 0)) >> CHUNK_SHIFT
    lane = lax.broadcasted_iota(I32, (tq, LANES), 1)

    def score_block(j, carry):
        k0 = pl.multiple_of(j * kb, kb)
        acc = [jnp.zeros((tq, LANES), F32) for _ in range(nc)]
        for h2 in range(IDX_HEADS // 2):
            qpair = qi_ref[0, :, h2 * LANES:(h2 + 1) * LANES]
            for par, kref in ((0, kie_ref), (1, kio_ref)):
                s = _dot(qpair, kref[0, :, pl.ds(k0, kb)])
                w = wb_s[2 * h2 + par]
                for c in range(nc):
                    acc[c] = acc[c] + jnp.maximum(s[:, c * LANES:(c + 1) * LANES], 0.0) * w
        for c in range(nc):
            bits = pltpu.bitcast(acc[c], I32)
            key = bits ^ ((bits >> 31) & 0x7FFFFFFF)
            kpos = k0 + c * LANES + lane
            key = jnp.where((kpos >> CHUNK_SHIFT) <= q_chunk, key, INT_MIN)
            key = jnp.where(kpos < l_real, key, INT_MIN)
            key_s[:, pl.ds(pl.multiple_of(k0 + c * LANES, LANES), LANES)] = key
        return carry

    lax.fori_loop(0, nkb, score_block, 0)

    def count_ge(cand):
        def body(j, cnt):
            k0 = pl.multiple_of(j * kb, kb)
            for c in range(nc):
                blk = key_s[:, pl.ds(pl.multiple_of(k0 + c * LANES, LANES), LANES)]
                cnt = cnt + jnp.where(blk >= cand, 1, 0)
            return cnt
        cnt = lax.fori_loop(0, nkb, body, jnp.zeros((tq, LANES), I32))
        return jnp.broadcast_to(jnp.sum(cnt, axis=1, keepdims=True), (tq, LANES))

    zero = jnp.zeros((tq, LANES), I32)
    res = jnp.where(count_ge(zero) >= n_sel, zero, INT_MIN)

    def bit_step(it, res):
        cand = res | (1 << (30 - it))
        return jnp.where(count_ge(cand) >= n_sel, cand, res)

    res = lax.fori_loop(0, 31, bit_step, res)
    thr = jnp.maximum(res, INT_MIN + 1)

    for g in range(N_KV_HEADS):
        for hh in range(KV_GROUP):
            h = g * KV_GROUP + hh
            qs_s[g, hh * tq:(hh + 1) * tq, :] = q_ref[0, :, h * HEAD_DIM:(h + 1) * HEAD_DIM]
    m_s[...] = jnp.full(m_s.shape, NEG, F32)
    l_s[...] = jnp.zeros(l_s.shape, F32)
    acc_s[...] = jnp.zeros(acc_s.shape, F32)
    scale = HEAD_DIM ** -0.5

    def attend(j, near):
        k0 = pl.multiple_of(j * kb, kb)
        mb = []
        for c in range(nc):
            blk = key_s[:, pl.ds(pl.multiple_of(k0 + c * LANES, LANES), LANES)]
            mb.append(jnp.where(blk >= thr, 0.0, NEG))
        for g in range(N_KV_HEADS):
            s_all = _dot(qs_s[g], kt_ref[0, g * HEAD_DIM:(g + 1) * HEAD_DIM, pl.ds(k0, kb)]) * scale
            vb = v_ref[0, pl.ds(k0, kb), g * HEAD_DIM:(g + 1) * HEAD_DIM]
            for hh in range(KV_GROUP):
                h = g * KV_GROUP + hh
                s = s_all[hh * tq:(hh + 1) * tq, :]
                sc = []
                for c in range(nc):
                    x = s[:, c * LANES:(c + 1) * LANES] + mb[c]
                    if near is not None:
                        x = x + nb_ref[near, h, :, c * LANES:(c + 1) * LANES]
                    sc.append(x)
                mx = sc[0]
                for c in range(1, nc):
                    mx = jnp.maximum(mx, sc[c])
                m_old = m_s[h]
                m_new = jnp.maximum(m_old, jnp.broadcast_to(jnp.max(mx, axis=1, keepdims=True), (tq, LANES)))
                alpha = jnp.exp(m_old - m_new)
                p = [jnp.exp(x - m_new) for x in sc]
                ps = p[0]
                for c in range(1, nc):
                    ps = ps + p[c]
                l_s[h] = alpha * l_s[h] + jnp.broadcast_to(jnp.sum(ps, axis=1, keepdims=True), (tq, LANES))
                m_s[h] = m_new
                pb = jnp.concatenate(p, axis=1).astype(BF16)
                acc_s[h] = alpha * acc_s[h] + _dot(pb, vb)

    def far_block(j, carry):
        attend(j, None)
        return carry

    lax.fori_loop(0, jnp.maximum(own - 1, 0), far_block, 0)

    @pl.when(own >= 1)
    def _():
        attend(own - 1, 0)

    attend(own, 1)

    for h in range(N_HEADS):
        o_ref[0, :, h * HEAD_DIM:(h + 1) * HEAD_DIM] = (acc_s[h] / l_s[h]).astype(BF16)


def _t5_bucket(rel):
    nb = NUM_BUCKETS // 2
    max_exact = nb // 2
    ret = jnp.where(rel > 0, nb, 0)
    n = jnp.abs(rel)
    n_f = jnp.maximum(n, 1).astype(F32)
    large = max_exact + (jnp.log(n_f / max_exact) / math.log(MAX_DISTANCE / max_exact)
                         * (nb - max_exact)).astype(I32)
    large = jnp.minimum(large, nb - 1)
    return ret + jnp.where(n < max_exact, n, large)


def _near_bias(rel_bias, tq):
    a = jnp.arange(tq, dtype=I32)[:, None]
    b = jnp.arange(DSA_BLK, dtype=I32)[None, :]
    centered = rel_bias - rel_bias[NUM_BUCKETS // 2 - 1]
    tiles = []
    for d in (-DSA_BLK, 0):
        bucket = jnp.where(d + b - a > -MAX_DISTANCE, _t5_bucket(d + b - a), NUM_BUCKETS // 2 - 1)
        t = jnp.zeros((N_HEADS, tq, DSA_BLK), F32)
        for c in range(NUM_BUCKETS):
            t = jnp.where((bucket == c)[None], centered[c][:, None, None], t)
        tiles.append(t)
    return jnp.stack(tiles)


def _dsa(q, qi, wi, kt, v, kie, kio, nbias, *, past, l_real, n_sel):
    b, t, _ = q.shape
    tq = min(t, DSA_BLK)
    lpad = kt.shape[2]
    tile = lambda w: pl.BlockSpec((1, tq, w), lambda i, j: (i, j, 0))
    per_b = lambda r, c: pl.BlockSpec((1, r, c), lambda i, j: (i, 0, 0), pipeline_mode=pl.Buffered(1))
    kvd = N_KV_HEADS * HEAD_DIM
    return pl.pallas_call(
        functools.partial(_dsa_kernel, tq=tq, past=past, l_real=l_real, n_sel=n_sel),
        grid=(b, t // tq),
        in_specs=[tile(D_MODEL), tile(IDX_HEADS * IDX_DIM), tile(IDX_HEADS),
                  per_b(kvd, lpad), per_b(lpad, kvd), per_b(LANES, lpad), per_b(LANES, lpad),
                  _const_spec(nbias.shape)],
        out_specs=tile(D_MODEL),
        out_shape=jax.ShapeDtypeStruct((b, t, D_MODEL), BF16),
        scratch_shapes=[pltpu.VMEM((tq, lpad), I32),
                        pltpu.VMEM((IDX_HEADS, tq, LANES), F32),
                        pltpu.VMEM((N_KV_HEADS, KV_GROUP * tq, HEAD_DIM), BF16),
                        pltpu.VMEM((N_HEADS, tq, LANES), F32),
                        pltpu.VMEM((N_HEADS, tq, LANES), F32),
                        pltpu.VMEM((N_HEADS, tq, HEAD_DIM), F32)],
        compiler_params=_params(("parallel", "arbitrary")),
        name="dsa",
    )(q, qi, wi, kt, v, kie, kio, nbias)


def _merge_kernel(x_ref, hg_ref, at_ref, ga_ref, gb_ref, wa_ref, wb_ref, wo_ref, gf_ref, wq_ref, sk_ref,
                  x1_ref, xn_ref, st_ref):
    ya = _dot(hg_ref[...], wa_ref[...])
    yb = _dot(at_ref[...], wb_ref[...])
    m = _sigmoid(ga_ref[...]) * ya + _sigmoid(gb_ref[...]) * yb
    x1 = x_ref[...] + _dot(m.astype(BF16), wo_ref[...])
    x1_ref[...] = x1
    xn = _rms(x1, gf_ref[...]).astype(BF16)
    xn_ref[...] = xn
    qp = _dot(xn, wq_ref[...])
    for j in range(2 * PEER_HEADS):
        qj = qp[:, j * LANES:(j + 1) * LANES].astype(BF16)
        st_ref[j] = _dot_nt(sk_ref[j], qj)


def _merge(x2, hg, attn, ga, gb, w_a_out, w_b_out, w_o, g_ffn, w_q, sk):
    n = x2.shape[0]
    tok = lambda w: pl.BlockSpec((TM, w), lambda i: (i, 0))
    sq = _const_spec((D_MODEL, D_MODEL))
    nsk = 2 * PEER_HEADS
    return pl.pallas_call(
        _merge_kernel,
        grid=(n // TM,),
        in_specs=[tok(D_MODEL)] * 5 + [sq, sq, sq, _const_spec((1, D_MODEL)),
                                       _const_spec(w_q.shape), _const_spec(sk.shape)],
        out_specs=[tok(D_MODEL), tok(D_MODEL), pl.BlockSpec((nsk, PEER_NKEYS, TM), lambda i: (0, 0, i))],
        out_shape=[jax.ShapeDtypeStruct((n, D_MODEL), F32), jax.ShapeDtypeStruct((n, D_MODEL), BF16),
                   jax.ShapeDtypeStruct((nsk, PEER_NKEYS, n), F32)],
        compiler_params=_params(("parallel",)),
        name="merge",
    )(x2, hg, attn, ga, gb, w_a_out, w_b_out, w_o, g_ffn, w_q, sk)


_PAIR_LIMIT = [PEER_TOPK // (i + 1) for i in range(PEER_TOPK)]
_NOT_TOP = 99.0


def _peer_select_kernel(st_ref, r2_ref, e2_ref, n1_ref, e1_ref):
    ninf = -jnp.inf
    sub = 8
    row8 = lax.broadcasted_iota(I32, (sub, PEER_TM), 0)

    def top(s):
        vals, cur = [], s
        rank = jnp.full(s.shape, _NOT_TOP, F32)
        for k in range(PEER_TOPK):
            m = jnp.max(cur, axis=0, keepdims=True)
            vals.append(m)
            sel = cur == m
            rank = jnp.where(sel, float(k), rank)
            cur = jnp.where(sel, ninf, cur)
        return vals, rank

    def head(h, carry):
        s1 = st_ref[2 * h]
        s2 = st_ref[2 * h + 1]
        v1, r1 = top(s1)
        v2, r2 = top(s2)
        v2t = jnp.concatenate(v2, axis=0)
        v1_low = jnp.concatenate(v1[sub:], axis=0)
        groups = [v1[0] + v2t]
        for i in range(1, sub):
            groups.append(jnp.where(row8 < _PAIR_LIMIT[i], v1[i] + v2t[:sub], ninf))
        groups.append(v1_low + v2[0])
        cand = jnp.concatenate(groups, axis=0)
        cur = cand
        for r in range(PEER_TOPK):
            thr = jnp.max(cur, axis=0, keepdims=True)
            if r + 1 < PEER_TOPK:
                cur = jnp.where(cur == thr, ninf, cur)
        picked = cand >= thr
        z = jnp.sum(jnp.where(picked, jnp.exp(cand - (v1[0] + v2[0])), 0.0), axis=0, keepdims=True)
        cnt = jnp.where(picked, 1.0, 0.0)
        n_of_rank = [jnp.sum(cnt[:PEER_TOPK], axis=0, keepdims=True)]
        for i in range(1, sub):
            lo = PEER_TOPK + (i - 1) * sub
            n_of_rank.append(jnp.sum(cnt[lo:lo + sub], axis=0, keepdims=True))
        lo = PEER_TOPK + (sub - 1) * sub
        for i in range(sub, PEER_TOPK):
            n_of_rank.append(cnt[lo + i - sub:lo + i - sub + 1])
        n1 = jnp.zeros(s1.shape, F32)
        for i in range(PEER_TOPK):
            n1 = jnp.where(r1 == float(i), n_of_rank[i], n1)
        r2_ref[h] = r2.astype(BF16)
        e2_ref[h] = (jnp.exp(s2 - v2[0]) / z).astype(BF16)
        n1_ref[h] = n1
        e1_ref[h] = jnp.exp(s1 - v1[0])
        return carry

    lax.fori_loop(0, PEER_HEADS, head, 0)


def _peer_select(st):
    n = st.shape[2]
    big = pl.BlockSpec((PEER_HEADS, PEER_NKEYS, PEER_TM), lambda i: (0, 0, i))
    sds = lambda dt: jax.ShapeDtypeStruct((PEER_HEADS, PEER_NKEYS, n), dt)
    return pl.pallas_call(
        _peer_select_kernel,
        grid=(n // PEER_TM,),
        in_specs=[pl.BlockSpec((2 * PEER_HEADS, PEER_NKEYS, PEER_TM), lambda i: (0, 0, i))],
        out_specs=[big, big, big, big],
        out_shape=[sds(BF16), sds(BF16), sds(F32), sds(F32)],
        compiler_params=_params(("parallel",)),
        name="peer_select",
    )(st)


def _gelu_folded(x):
    c = math.sqrt(2.0 / math.pi)
    half = 0.5 * x
    return half + half * jnp.tanh(x * (c + (c * 0.044715) * (x * x)))


def _peer_dense_kernel(xnt_ref, u_ref, vt_ref, r2_ref, e2_ref, n1_ref, e1_ref, o_ref, acc_s):
    e = pl.program_id(1)
    pack = 16
    n_pack = PEER_NKEYS // pack
    per_chunk = PEER_EB // PEER_NKEYS

    @pl.when(e == 0)
    def _():
        acc_s[...] = jnp.zeros(acc_s.shape, F32)

    def gate_weights(a):
        w = [jnp.zeros((pack, PEER_TM), BF16) for _ in range(n_pack)]
        for h in range(PEER_HEADS):
            n1 = jnp.broadcast_to(n1_ref[h, pl.ds(a, 1), :], (pack, PEER_TM)).astype(BF16)
            e1 = jnp.broadcast_to(e1_ref[h, pl.ds(a, 1), :], (pack, PEER_TM)).astype(BF16)
            for r in range(n_pack):
                rows = slice(r * pack, (r + 1) * pack)
                w[r] = w[r] + jnp.where(r2_ref[h, rows, :] < n1, e2_ref[h, rows, :], 0.0) * e1
        return jnp.concatenate(w, axis=0)

    xnt = xnt_ref[...]
    total = None
    for k in range(PEER_CHUNKS):
        a0 = (e * PEER_CHUNKS + k) * per_chunk
        w = [gate_weights(a0 + aa) for aa in range(per_chunk)]
        act_t = _dot(u_ref[k * PEER_EB:(k + 1) * PEER_EB, :], xnt)
        g = [w[aa] * _gelu_folded(act_t[aa * PEER_NKEYS:(aa + 1) * PEER_NKEYS, :]).astype(BF16)
             for aa in range(per_chunk)]
        part = _dot(vt_ref[:, k * PEER_EB:(k + 1) * PEER_EB], jnp.concatenate(g, axis=0))
        total = part if total is None else total + part
    acc_s[...] += total

    @pl.when(e == pl.num_programs(1) - 1)
    def _():
        o_ref[...] = acc_s[...].T


def _peer_dense(xnt, u_b, vt_b, r2, e2, n1, e1):
    n = xnt.shape[1]
    step = PEER_EB * PEER_CHUNKS
    big = pl.BlockSpec((PEER_HEADS, PEER_NKEYS, PEER_TM), lambda i, e: (0, 0, i))
    return pl.pallas_call(
        _peer_dense_kernel,
        grid=(n // PEER_TM, PEER_EXPERTS // step),
        in_specs=[pl.BlockSpec((D_MODEL, PEER_TM), lambda i, e: (0, i)),
                  pl.BlockSpec((step, D_MODEL), lambda i, e: (e, 0)),
                  pl.BlockSpec((D_MODEL, step), lambda i, e: (0, e)),
                  big, big, big, big],
        out_specs=pl.BlockSpec((PEER_TM, D_MODEL), lambda i, e: (i, 0)),
        out_shape=jax.ShapeDtypeStruct((n, D_MODEL), F32),
        scratch_shapes=[pltpu.VMEM((D_MODEL, PEER_TM), F32)],
        compiler_params=_params(("parallel", "arbitrary")),
        name="peer_dense",
    )(xnt, u_b, vt_b, r2, e2, n1, e1)


def _final_kernel(x1_ref, po_ref, p_ref, gp_ref, wg_ref, wp_ref, gfin_ref, y_ref):
    x2 = x1_ref[...] + po_ref[...]
    gate = _sigmoid(_dot(_rms(x2, gp_ref[...]).astype(BF16), wg_ref[...]))
    x3 = x2 + gate * _dot(p_ref[...].astype(BF16), wp_ref[...])
    y_ref[...] = _rms(x3, gfin_ref[...])


def _final(x1, po, p2, g_ple, w_gate, w_proj, g_final):
    n = x1.shape[0]
    tok = lambda w: pl.BlockSpec((TM, w), lambda i: (i, 0))
    vec = _const_spec((1, D_MODEL))
    return pl.pallas_call(
        _final_kernel,
        grid=(n // TM,),
        in_specs=[tok(D_MODEL), tok(D_MODEL), tok(D_PLE), vec, _const_spec((D_MODEL, D_MODEL)),
                  _const_spec((D_PLE, D_MODEL)), vec],
        out_specs=tok(D_MODEL),
        out_shape=jax.ShapeDtypeStruct((n, D_MODEL), F32),
        compiler_params=_params(("parallel",)),
        name="final",
    )(x1, po, p2, g_ple, w_gate, w_proj, g_final)


def _pad_keys(x, lpad, axis):
    pad = lpad - x.shape[axis]
    if pad == 0:
        return x
    widths = [(0, 0)] * x.ndim
    widths[axis] = (0, pad)
    return jnp.pad(x, widths)


def _layer(x, p, conv_state, h_state, cache_k, cache_v, cache_ki, w):
    b, t, _ = x.shape
    n = b * t
    kvd = N_KV_HEADS * HEAD_DIM
    (rx, rgate, q, k, v, kb, vb, qi, ga, gb, ki, kib, wi) = _inproj(x.reshape(n, D_MODEL), w["g_mix"], w["w_in"])

    hg, conv_new, h_last = _rglru(rx.reshape(b, t, D_RNN), rgate.reshape(b, t, D_RNN), conv_state,
                                  h_state.reshape(b, 1, D_RNN), w["conv_w"], w["conv_b"], w["w_rg_a"],
                                  w["b_rg_a"], w["w_rg_x"], w["b_rg_x"], w["rg_lambda"])

    k_all, v_all, ki_all = kb.reshape(b, t, kvd), vb.reshape(b, t, kvd), kib.reshape(b, t, IDX_DIM)
    past = 0
    if cache_k is not None:
        past = cache_k.shape[1]
        k_all = jnp.concatenate([cache_k.reshape(b, past, kvd).astype(BF16), k_all], axis=1)
        v_all = jnp.concatenate([cache_v.reshape(b, past, kvd).astype(BF16), v_all], axis=1)
        ki_all = jnp.concatenate([cache_ki.astype(BF16), ki_all], axis=1)
    l_real = past + t
    assert past % DSA_BLK == 0 and t % min(t, DSA_BLK) == 0 and n % PEER_TM == 0
    lpad = -(-l_real // DSA_BLK) * DSA_BLK
    kt = _pad_keys(jnp.swapaxes(k_all, 1, 2), lpad, 2)
    v_all = _pad_keys(v_all, lpad, 1)
    kit = _pad_keys(jnp.swapaxes(ki_all, 1, 2), lpad, 2)
    zeros = jnp.zeros_like(kit)
    kie = jnp.concatenate([kit, zeros], axis=1)
    kio = jnp.concatenate([zeros, kit], axis=1)
    tq = min(t, DSA_BLK)
    attn = _dsa(q.reshape(b, t, D_MODEL), qi.reshape(b, t, IDX_HEADS * IDX_DIM), wi.reshape(b, t, IDX_HEADS),
                kt, v_all, kie, kio, _near_bias(w["rel_bias"], tq),
                past=past, l_real=l_real, n_sel=min(TOPK_MAX, l_real // 4))

    x1, xn, st = _merge(x.reshape(n, D_MODEL), hg.reshape(n, D_RNN), attn.reshape(n, D_MODEL), ga, gb,
                        w["w_a_out"], w["w_b_out"], w["w_o"], w["g_ffn"], w["w_peer_q"], w["peer_sk"])
    r2, e2, n1, e1 = _peer_select(st)
    po = _peer_dense(xn.T, w["peer_u"], w["peer_vt"], r2, e2, n1, e1)
    y = _final(x1, po, p.reshape(n, D_PLE), w["g_ple"], w["w_ple_gate"], w["w_ple_proj"], w["g_final"])
    return (y.reshape(b, t, D_MODEL), k.reshape(b, t, N_KV_HEADS, HEAD_DIM),
            v.reshape(b, t, N_KV_HEADS, HEAD_DIM), ki.reshape(b, t, IDX_DIM), conv_new,
            h_last.reshape(b, D_RNN))


def kernel(x_prompt, x_sample, p_prompt, p_sample, state_conv, state_rglru, cache_k, cache_v, cache_idx_k, rel_bias, g_mix, w_in, conv_w, conv_b, w_rg_a, b_rg_a, w_rg_x, b_rg_x, rg_lambda, w_a_out, w_b_out, w_o, g_ffn, w_peer_q, peer_sub_keys, peer_u, peer_v, g_ple, w_ple_gate, w_ple_proj, g_final):
    assert g_mix.shape[0] == 1, "single trunk layer"
    row = lambda a: a.reshape(1, -1)
    wi_full = w_in[0]
    cut = _C_GA
    tail = IDX_DIM + IDX_HEADS
    w_r = jnp.concatenate([wi_full[:, :cut], wi_full[:, cut + tail:], wi_full[:, cut:cut + tail],
                           jnp.zeros((D_MODEL, _C_END - _C_TAIL - tail), F32)], axis=1).astype(BF16)
    w = dict(
        g_mix=row(g_mix[0]), w_in=w_r, conv_w=conv_w[0], conv_b=row(conv_b[0]),
        w_rg_a=w_rg_a[0].astype(BF16), b_rg_a=row(b_rg_a[0]), w_rg_x=w_rg_x[0].astype(BF16),
        b_rg_x=row(b_rg_x[0]), rg_lambda=row(rg_lambda[0]), rel_bias=rel_bias,
        w_a_out=w_a_out[0].astype(BF16), w_b_out=w_b_out[0].astype(BF16), w_o=w_o[0].astype(BF16),
        g_ffn=row(g_ffn[0]), w_peer_q=w_peer_q[0].astype(BF16),
        peer_sk=peer_sub_keys[0].reshape(2 * PEER_HEADS, PEER_NKEYS, PEER_DK // 2).astype(BF16),
        peer_u=peer_u[0].astype(BF16), peer_vt=peer_v[0].astype(BF16).T,
        g_ple=row(g_ple[0]), w_ple_gate=w_ple_gate[0].astype(BF16), w_ple_proj=w_ple_proj[0].astype(BF16),
        g_final=row(g_final),
    )
    bp = x_prompt.shape[0]
    zc = jnp.zeros((bp, CONV_W - 1, D_RNN), F32)
    zh = jnp.zeros((bp, D_RNN), F32)
    yp, k1, v1, ki1, c1, r1 = _layer(x_prompt, p_prompt[0], zc, zh, None, None, None, w)
    ys, k2, v2, ki2, c2, r2 = _layer(x_sample, p_sample[0], state_conv[0], state_rglru[0],
                                     cache_k[0], cache_v[0], cache_idx_k[0], w)
    return (yp, ys, k1[None], v1[None], ki1[None], c1[None], r1[None],
            k2[None], v2[None], ki2[None], c2[None], r2[None])
```

```python
import functools
import math

import jax
import jax.numpy as jnp
from jax import lax
from jax.experimental import pallas as pl
from jax.experimental.pallas import tpu as pltpu

F32 = jnp.float32
BF16 = jnp.bfloat16
I32 = jnp.int32
I16 = jnp.int16
I16_MIN = -(2 ** 15)
LOG2E = math.log2(math.e)

D_MODEL = 1024
CHUNK = 64
CHUNK_SHIFT = 6
D_PLE = 256
D_RNN = 1024
RG_BLOCKS = 8
RG_BLOCK = D_RNN // RG_BLOCKS
CONV_W = 4
RG_C = 8.0
N_HEADS = 8
HEAD_DIM = 128
N_KV_HEADS = 2
KV_GROUP = N_HEADS // N_KV_HEADS
IDX_HEADS = 16
IDX_DIM = 64
TOPK_MAX = 256
NUM_BUCKETS = 32
MAX_DISTANCE = 128
PEER_HEADS = 8
PEER_NKEYS = 128
PEER_EXPERTS = PEER_NKEYS * PEER_NKEYS
PEER_DK = 256
PEER_TOPK = 16
EPS = 1e-6
NEG = -1e30
INT_MIN = -(2 ** 31)

LANES = 128
VMEM_LIMIT = 56 * 1024 * 1024

_C_RX, _C_RG, _C_Q, _C_K, _C_V, _C_QI, _C_GA, _C_GB, _C_TAIL, _C_END = (
    0, 1024, 2048, 3072, 3328, 3584, 4608, 5632, 6656, 6784)

TM = 256
DSA_BLK = 256
DSA_WIDE = 1024
PEER_TM = 512
PEER_EB = 512
PEER_CHUNKS = 4


def _params(sem):
    return pltpu.CompilerParams(dimension_semantics=sem, vmem_limit_bytes=VMEM_LIMIT)


def _const_spec(shape):
    nd = len(shape)
    return pl.BlockSpec(shape, lambda *_: (0,) * nd, pipeline_mode=pl.Buffered(1))


def _rms(x, g):
    return x * lax.rsqrt(jnp.mean(x * x, axis=-1, keepdims=True) + EPS) * g


def _gelu(x):
    return 0.5 * x * (1.0 + jnp.tanh(math.sqrt(2.0 / math.pi) * (x + 0.044715 * (x * x * x))))


def _sigmoid(x):
    return 1.0 / (1.0 + jnp.exp(-x))


def _dot(a, b):
    return jnp.dot(a, b, preferred_element_type=F32)


def _dot_nt(a, b):
    return lax.dot_general(a, b, (((1,), (1,)), ((), ())), preferred_element_type=F32)


def _inproj_kernel(x_ref, g_ref, w_ref, rx_ref, rg_ref, q_ref, k_ref, v_ref, kb_ref, vb_ref,
                   qi_ref, ga_ref, gb_ref, ki_ref, kib_ref, wi_ref):
    n = _rms(x_ref[...], g_ref[...]).astype(BF16)

    def mm(a, b):
        return _dot(n, w_ref[:, a:b])

    rx_ref[...] = mm(_C_RX, _C_RG)
    rg_ref[...] = mm(_C_RG, _C_Q)
    q_ref[...] = mm(_C_Q, _C_K).astype(BF16)
    k = mm(_C_K, _C_V)
    k_ref[...] = k
    kb_ref[...] = k.astype(BF16)
    v = mm(_C_V, _C_QI)
    v_ref[...] = v
    vb_ref[...] = v.astype(BF16)
    qi_ref[...] = mm(_C_QI, _C_GA).astype(BF16)
    ga_ref[...] = mm(_C_GA, _C_GB)
    gb_ref[...] = mm(_C_GB, _C_TAIL)
    tail = mm(_C_TAIL, _C_END)
    ki = tail[:, :IDX_DIM]
    ki_ref[...] = ki
    kib_ref[...] = ki.astype(BF16)
    wi_ref[...] = tail[:, IDX_DIM:IDX_DIM + IDX_HEADS]


def _inproj(x2, g_mix, w_r):
    n = x2.shape[0]
    kvd = N_KV_HEADS * HEAD_DIM
    tok = lambda w: pl.BlockSpec((TM, w), lambda i: (i, 0))
    widths_dtypes = [(D_RNN, F32), (D_RNN, F32), (D_MODEL, BF16), (kvd, F32), (kvd, F32), (kvd, BF16),
                     (kvd, BF16), (IDX_HEADS * IDX_DIM, BF16), (D_MODEL, F32), (D_MODEL, F32),
                     (IDX_DIM, F32), (IDX_DIM, BF16), (IDX_HEADS, F32)]
    return pl.pallas_call(
        _inproj_kernel,
        grid=(n // TM,),
        in_specs=[tok(D_MODEL), _const_spec((1, D_MODEL)), _const_spec(w_r.shape)],
        out_specs=[tok(w) for w, _ in widths_dtypes],
        out_shape=[jax.ShapeDtypeStruct((n, w), dt) for w, dt in widths_dtypes],
        compiler_params=_params(("parallel",)),
        name="inproj",
    )(x2, g_mix, w_r)


def _rglru_kernel(x_ref, gate_ref, cs_ref, h0_ref, cw_ref, cb_ref, wa_ref, ba_ref, wx_ref, bx_ref,
                  lam_ref, hg_ref, cnew_ref, hlast_ref, xp_s, hc_s, *, tt):
    @pl.when(pl.program_id(1) == 0)
    def _():
        xp_s[5:8, :] = cs_ref[0]
        hc_s[...] = h0_ref[0]

    xp_s[8:8 + tt, :] = x_ref[0]
    nl = -lam_ref[...]
    softplus = jnp.maximum(nl, 0.0) + jnp.log1p(jnp.exp(-jnp.abs(nl)))
    row = lax.broadcasted_iota(I32, (tt, RG_BLOCK), 0)
    for n in range(RG_BLOCKS):
        sl = slice(n * RG_BLOCK, (n + 1) * RG_BLOCK)
        xc = cb_ref[:, sl]
        for j in range(CONV_W):
            xc = xc + cw_ref[j:j + 1, sl] * xp_s[5 + j:5 + j + tt, sl]
        xcb = xc.astype(BF16)
        r = _sigmoid(_dot(xcb, wa_ref[n]) + ba_ref[:, sl])
        ig = _sigmoid(_dot(xcb, wx_ref[n]) + bx_ref[:, sl])
        log_a = -RG_C * r * softplus[:, sl]
        a = jnp.exp(log_a)
        mult = jnp.sqrt(-jnp.tanh(log_a) * (a * a + 1.0))
        bv = mult * (ig * xc)
        d = 1
        while d < tt:
            keep = row >= d
            a_sh = pltpu.roll(a, d, 0)
            b_sh = pltpu.roll(bv, d, 0)
            bv = jnp.where(keep, a * b_sh + bv, bv)
            a = jnp.where(keep, a * a_sh, a)
            d *= 2
        h = a * hc_s[:, sl] + bv
        hc_s[:, sl] = h[tt - 1:tt, :]
        hg_ref[0, :, sl] = (h * _gelu(gate_ref[0, :, sl])).astype(BF16)
    tail = xp_s[tt + 5:tt + 8, :]
    cnew_ref[0] = tail
    xp_s[5:8, :] = tail
    hlast_ref[0] = hc_s[...]


def _rglru(rx, rgate, conv_state, h0, conv_w, conv_b, w_a, b_a, w_x, b_x, lam):
    b, t, _ = rx.shape
    tt = min(t, 256)
    seq = pl.BlockSpec((1, tt, D_RNN), lambda i, j: (i, j, 0))
    per_b = lambda r: pl.BlockSpec((1, r, D_RNN), lambda i, j: (i, 0, 0))
    vec = _const_spec((1, D_RNN))
    wblk = _const_spec((RG_BLOCKS, RG_BLOCK, RG_BLOCK))
    return pl.pallas_call(
        functools.partial(_rglru_kernel, tt=tt),
        grid=(b, t // tt),
        in_specs=[seq, seq, per_b(CONV_W - 1), per_b(1), _const_spec((CONV_W, D_RNN)), vec,
                  wblk, vec, wblk, vec, vec],
        out_specs=[seq, per_b(CONV_W - 1), per_b(1)],
        out_shape=[jax.ShapeDtypeStruct((b, t, D_RNN), BF16),
                   jax.ShapeDtypeStruct((b, CONV_W - 1, D_RNN), F32),
                   jax.ShapeDtypeStruct((b, 1, D_RNN), F32)],
        scratch_shapes=[pltpu.VMEM((tt + 8, D_RNN), F32), pltpu.VMEM((1, D_RNN), F32)],
        compiler_params=_params(("parallel", "arbitrary")),
        name="rglru",
    )(rx, rgate, conv_state, h0, conv_w, conv_b, w_a, b_a, w_x, b_x, lam)


def _dsa_kernel(q_ref, qi_ref, wi_ref, kt_ref, v_ref, kie_ref, kio_ref, nb_ref, o_ref,
                key_s, hi_s, lo_s, wb_s, qs_s, m_s, l_s, acc_s, *, tq, past, l_real, n_sel):
    kb = DSA_BLK
    nc = kb // LANES
    i = pl.program_id(1)
    q0 = past + i * tq
    own = q0 // kb
    nkb = own + 1
    n_unit = (nkb + 1) // 2

    def lanes_at(k0):
        return pl.ds(pl.multiple_of(k0, LANES), LANES)

    wi = wi_ref[0] * (IDX_HEADS ** -0.5 * IDX_DIM ** -0.5)
    for h in range(IDX_HEADS):
        wb_s[h] = jnp.broadcast_to(wi[:, h:h + 1], (tq, LANES))
    q_chunk = (q0 + lax.broadcasted_iota(I32, (tq, LANES), 0)) >> CHUNK_SHIFT
    lane = lax.broadcasted_iota(I32, (tq, LANES), 1)

    def score_block(j, carry):
        k0 = pl.multiple_of(j * kb, kb)
        acc = [jnp.zeros((tq, LANES), F32) for _ in range(nc)]
        for h2 in range(IDX_HEADS // 2):
            qpair = qi_ref[0, :, h2 * LANES:(h2 + 1) * LANES]
            for par, kref in ((0, kie_ref), (1, kio_ref)):
                s = _dot(qpair, kref[0, :, pl.ds(k0, kb)])
                w = wb_s[2 * h2 + par]
                for c in range(nc):
                    acc[c] = acc[c] + jnp.maximum(s[:, c * LANES:(c + 1) * LANES], 0.0) * w
        for c in range(nc):
            bits = pltpu.bitcast(acc[c], I32)
            key = bits ^ ((bits >> 31) & 0x7FFFFFFF)
            kpos = k0 + c * LANES + lane
            key = jnp.where((kpos >> CHUNK_SHIFT) <= q_chunk, key, INT_MIN)
            key = jnp.where(kpos < l_real, key, INT_MIN)
            at = lanes_at(k0 + c * LANES)
            key_s[:, at] = key
            hi_s[:, at] = (key >> 16).astype(I16)
            lo_s[:, at] = ((key & 0xFFFF) + I16_MIN).astype(I16)
        return carry

    lax.fori_loop(0, nkb, score_block, 0)

    @pl.when(nkb % 2 == 1)
    def _():
        for c in range(nc):
            at = lanes_at(nkb * kb + c * LANES)
            hi_s[:, at] = jnp.full((tq, LANES), I16_MIN, I16)
            lo_s[:, at] = jnp.full((tq, LANES), I16_MIN, I16)

    unit = 2 * kb
    one16 = jnp.ones((tq, LANES), I16)
    zero16 = jnp.zeros((tq, LANES), I16)

    def count(ref, cand, strict):
        cand16 = cand.astype(I16)

        def body(u, cnt):
            k0 = u * unit
            for c in range(unit // LANES):
                blk = ref[:, lanes_at(k0 + c * LANES)]
                hit = (blk > cand16) if strict else (blk >= cand16)
                cnt = cnt + jnp.where(hit, one16, zero16)
            return cnt

        cnt = lax.fori_loop(0, n_unit, body, zero16)
        return jnp.broadcast_to(jnp.sum(cnt.astype(I32), axis=1, keepdims=True), (tq, LANES))

    def search16(ref, target):
        zero = jnp.zeros((tq, LANES), I32)
        res = jnp.where(count(ref, zero, False) >= target, zero, I16_MIN)

        def bit_step(it, res):
            cand = res | (1 << (14 - it))
            return jnp.where(count(ref, cand, False) >= target, cand, res)

        return lax.fori_loop(0, 15, bit_step, res)

    p_hi = search16(hi_s, n_sel)
    target_lo = n_sel - count(hi_s, p_hi, True)
    p_hi16 = p_hi.astype(I16)

    def keep_group(u, carry):
        k0 = u * unit
        for c in range(unit // LANES):
            at = lanes_at(k0 + c * LANES)
            lo_s[:, at] = jnp.where(hi_s[:, at] == p_hi16, lo_s[:, at], jnp.full((tq, LANES), I16_MIN, I16))
        return carry

    lax.fori_loop(0, n_unit, keep_group, 0)
    p_lo = search16(lo_s, target_lo)
    thr = jnp.maximum(p_hi * 65536 + (p_lo - I16_MIN), INT_MIN + 1)

    for g in range(N_KV_HEADS):
        for hh in range(KV_GROUP):
            h = g * KV_GROUP + hh
            qs_s[g, hh * tq:(hh + 1) * tq, :] = q_ref[0, :, h * HEAD_DIM:(h + 1) * HEAD_DIM]
    m_s[...] = jnp.full(m_s.shape, NEG, F32)
    l_s[...] = jnp.zeros(l_s.shape, F32)
    acc_s[...] = jnp.zeros(acc_s.shape, F32)
    scale2 = HEAD_DIM ** -0.5 * LOG2E

    def attend(k0, width, near):
        ncw = width // LANES
        mb = [jnp.where(key_s[:, lanes_at(k0 + c * LANES)] >= thr, 0.0, NEG) for c in range(ncw)]
        for g in range(N_KV_HEADS):
            s_all = _dot(qs_s[g], kt_ref[0, g * HEAD_DIM:(g + 1) * HEAD_DIM, pl.ds(k0, width)]) * scale2
            vb = v_ref[0, pl.ds(k0, width), g * HEAD_DIM:(g + 1) * HEAD_DIM]
            for hh in range(KV_GROUP):
                h = g * KV_GROUP + hh
                s = s_all[hh * tq:(hh + 1) * tq, :]
                sc = []
                for c in range(ncw):
                    x = s[:, c * LANES:(c + 1) * LANES] + mb[c]
                    if near is not None:
                        x = x + nb_ref[near, h, :, c * LANES:(c + 1) * LANES]
                    sc.append(x)
                mx = sc[0]
                for c in range(1, ncw):
                    mx = jnp.maximum(mx, sc[c])
                m_old = m_s[h]
                m_new = jnp.maximum(m_old, jnp.broadcast_to(jnp.max(mx, axis=1, keepdims=True), (tq, LANES)))
                alpha = jnp.exp2(m_old - m_new)
                p = [jnp.exp2(x - m_new) for x in sc]
                ps = p[0]
                for c in range(1, ncw):
                    ps = ps + p[c]
                l_s[h] = alpha * l_s[h] + jnp.broadcast_to(jnp.sum(ps, axis=1, keepdims=True), (tq, LANES))
                m_s[h] = m_new
                pb = jnp.concatenate(p, axis=1).astype(BF16)
                acc_s[h] = alpha * acc_s[h] + _dot(pb, vb)

    n_far = jnp.maximum(own - 1, 0)
    per_wide = DSA_WIDE // kb
    n_wide = n_far // per_wide

    def wide_step(j, carry):
        attend(pl.multiple_of(j * DSA_WIDE, DSA_WIDE), DSA_WIDE, None)
        return carry

    def far_block(j, carry):
        attend(pl.multiple_of(j * kb, kb), kb, None)
        return carry

    lax.fori_loop(0, n_wide, wide_step, 0)
    lax.fori_loop(n_wide * per_wide, n_far, far_block, 0)

    @pl.when(own >= 1)
    def _():
        attend(pl.multiple_of((own - 1) * kb, kb), kb, 0)

    attend(pl.multiple_of(own * kb, kb), kb, 1)

    for h in range(N_HEADS):
        o_ref[0, :, h * HEAD_DIM:(h + 1) * HEAD_DIM] = (acc_s[h] / l_s[h]).astype(BF16)


def _t5_bucket(rel):
    nb = NUM_BUCKETS // 2
    max_exact = nb // 2
    ret = jnp.where(rel > 0, nb, 0)
    n = jnp.abs(rel)
    n_f = jnp.maximum(n, 1).astype(F32)
    large = max_exact + (jnp.log(n_f / max_exact) / math.log(MAX_DISTANCE / max_exact)
                         * (nb - max_exact)).astype(I32)
    large = jnp.minimum(large, nb - 1)
    return ret + jnp.where(n < max_exact, n, large)


def _near_bias(rel_bias, tq):
    a = jnp.arange(tq, dtype=I32)[:, None]
    b = jnp.arange(DSA_BLK, dtype=I32)[None, :]
    centered = (rel_bias - rel_bias[NUM_BUCKETS // 2 - 1]) * LOG2E
    tiles = []
    for d in (-DSA_BLK, 0):
        bucket = jnp.where(d + b - a > -MAX_DISTANCE, _t5_bucket(d + b - a), NUM_BUCKETS // 2 - 1)
        t = jnp.zeros((N_HEADS, tq, DSA_BLK), F32)
        for c in range(NUM_BUCKETS):
            t = jnp.where((bucket == c)[None], centered[c][:, None, None], t)
        tiles.append(t)
    return jnp.stack(tiles)


def _dsa(q, qi, wi, kt, v, kie, kio, nbias, *, past, l_real, n_sel):
    b, t, _ = q.shape
    tq = min(t, DSA_BLK)
    lpad = kt.shape[2]
    tile = lambda w: pl.BlockSpec((1, tq, w), lambda i, j: (i, j, 0))
    per_b = lambda r, c: pl.BlockSpec((1, r, c), lambda i, j: (i, 0, 0), pipeline_mode=pl.Buffered(1))
    kvd = N_KV_HEADS * HEAD_DIM
    return pl.pallas_call(
        functools.partial(_dsa_kernel, tq=tq, past=past, l_real=l_real, n_sel=n_sel),
        grid=(b, t // tq),
        in_specs=[tile(D_MODEL), tile(IDX_HEADS * IDX_DIM), tile(IDX_HEADS),
                  per_b(kvd, lpad), per_b(lpad, kvd), per_b(LANES, lpad), per_b(LANES, lpad),
                  _const_spec(nbias.shape)],
        out_specs=tile(D_MODEL),
        out_shape=jax.ShapeDtypeStruct((b, t, D_MODEL), BF16),
        scratch_shapes=[pltpu.VMEM((tq, lpad), I32),
                        pltpu.VMEM((tq, lpad), I16),
                        pltpu.VMEM((tq, lpad), I16),
                        pltpu.VMEM((IDX_HEADS, tq, LANES), F32),
                        pltpu.VMEM((N_KV_HEADS, KV_GROUP * tq, HEAD_DIM), BF16),
                        pltpu.VMEM((N_HEADS, tq, LANES), F32),
                        pltpu.VMEM((N_HEADS, tq, LANES), F32),
                        pltpu.VMEM((N_HEADS, tq, HEAD_DIM), F32)],
        compiler_params=_params(("parallel", "arbitrary")),
        name="dsa",
    )(q, qi, wi, kt, v, kie, kio, nbias)


def _merge_kernel(x_ref, hg_ref, at_ref, ga_ref, gb_ref, wa_ref, wb_ref, wo_ref, gf_ref, wq_ref, sk_ref,
                  x1_ref, xn_ref, st_ref):
    ya = _dot(hg_ref[...], wa_ref[...])
    yb = _dot(at_ref[...], wb_ref[...])
    m = _sigmoid(ga_ref[...]) * ya + _sigmoid(gb_ref[...]) * yb
    x1 = x_ref[...] + _dot(m.astype(BF16), wo_ref[...])
    x1_ref[...] = x1
    xn = _rms(x1, gf_ref[...]).astype(BF16)
    xn_ref[...] = xn
    qp = _dot(xn, wq_ref[...])
    for j in range(2 * PEER_HEADS):
        qj = qp[:, j * LANES:(j + 1) * LANES].astype(BF16)
        st_ref[j] = _dot_nt(sk_ref[j], qj)


def _merge(x2, hg, attn, ga, gb, w_a_out, w_b_out, w_o, g_ffn, w_q, sk):
    n = x2.shape[0]
    tok = lambda w: pl.BlockSpec((TM, w), lambda i: (i, 0))
    sq = _const_spec((D_MODEL, D_MODEL))
    nsk = 2 * PEER_HEADS
    return pl.pallas_call(
        _merge_kernel,
        grid=(n // TM,),
        in_specs=[tok(D_MODEL)] * 5 + [sq, sq, sq, _const_spec((1, D_MODEL)),
                                       _const_spec(w_q.shape), _const_spec(sk.shape)],
        out_specs=[tok(D_MODEL), tok(D_MODEL), pl.BlockSpec((nsk, PEER_NKEYS, TM), lambda i: (0, 0, i))],
        out_shape=[jax.ShapeDtypeStruct((n, D_MODEL), F32), jax.ShapeDtypeStruct((n, D_MODEL), BF16),
                   jax.ShapeDtypeStruct((nsk, PEER_NKEYS, n), F32)],
        compiler_params=_params(("parallel",)),
        name="merge",
    )(x2, hg, attn, ga, gb, w_a_out, w_b_out, w_o, g_ffn, w_q, sk)


_PAIR_LIMIT = [PEER_TOPK // (i + 1) for i in range(PEER_TOPK)]
_NOT_TOP = 99.0


def _peer_select_kernel(st_ref, r2_ref, e2_ref, n1_ref, e1_ref):
    ninf = -jnp.inf
    sub = 8
    row8 = lax.broadcasted_iota(I32, (sub, PEER_TM), 0)

    def top(s):
        vals, cur = [], s
        rank = jnp.full(s.shape, _NOT_TOP, F32)
        for k in range(PEER_TOPK):
            m = jnp.max(cur, axis=0, keepdims=True)
            vals.append(m)
            sel = cur == m
            rank = jnp.where(sel, float(k), rank)
            cur = jnp.where(sel, ninf, cur)
        return vals, rank

    def head(h, carry):
        s1 = st_ref[2 * h]
        s2 = st_ref[2 * h + 1]
        v1, r1 = top(s1)
        v2, r2 = top(s2)
        v2t = jnp.concatenate(v2, axis=0)
        v1_low = jnp.concatenate(v1[sub:], axis=0)
        groups = [v1[0] + v2t]
        for i in range(1, sub):
            groups.append(jnp.where(row8 < _PAIR_LIMIT[i], v1[i] + v2t[:sub], ninf))
        groups.append(v1_low + v2[0])
        cand = jnp.concatenate(groups, axis=0)
        cur = cand
        for r in range(PEER_TOPK):
            thr = jnp.max(cur, axis=0, keepdims=True)
            if r + 1 < PEER_TOPK:
                cur = jnp.where(cur == thr, ninf, cur)
        picked = cand >= thr
        z = jnp.sum(jnp.where(picked, jnp.exp(cand - (v1[0] + v2[0])), 0.0), axis=0, keepdims=True)
        cnt = jnp.where(picked, 1.0, 0.0)
        n_of_rank = [jnp.sum(cnt[:PEER_TOPK], axis=0, keepdims=True)]
        for i in range(1, sub):
            lo = PEER_TOPK + (i - 1) * sub
            n_of_rank.append(jnp.sum(cnt[lo:lo + sub], axis=0, keepdims=True))
        lo = PEER_TOPK + (sub - 1) * sub
        for i in range(sub, PEER_TOPK):
            n_of_rank.append(cnt[lo + i - sub:lo + i - sub + 1])
        n1 = jnp.zeros(s1.shape, F32)
        for i in range(PEER_TOPK):
            n1 = jnp.where(r1 == float(i), n_of_rank[i], n1)
        r2_ref[h] = r2.astype(BF16)
        e2_ref[h] = (jnp.exp(s2 - v2[0]) / z).astype(BF16)
        n1_ref[h] = n1
        e1_ref[h] = jnp.exp(s1 - v1[0])
        return carry

    lax.fori_loop(0, PEER_HEADS, head, 0)


def _peer_select(st):
    n = st.shape[2]
    big = pl.BlockSpec((PEER_HEADS, PEER_NKEYS, PEER_TM), lambda i: (0, 0, i))
    sds = lambda dt: jax.ShapeDtypeStruct((PEER_HEADS, PEER_NKEYS, n), dt)
    return pl.pallas_call(
        _peer_select_kernel,
        grid=(n // PEER_TM,),
        in_specs=[pl.BlockSpec((2 * PEER_HEADS, PEER_NKEYS, PEER_TM), lambda i: (0, 0, i))],
        out_specs=[big, big, big, big],
        out_shape=[sds(BF16), sds(BF16), sds(F32), sds(F32)],
        compiler_params=_params(("parallel",)),
        name="peer_select",
    )(st)


def _gelu_folded(x):
    c = math.sqrt(2.0 / math.pi)
    half = 0.5 * x
    return half + half * jnp.tanh(x * (c + (c * 0.044715) * (x * x)))


def _peer_dense_kernel(xnt_ref, u_ref, vt_ref, r2_ref, e2_ref, n1_ref, e1_ref, o_ref, acc_s):
    e = pl.program_id(1)
    pack = 16
    n_pack = PEER_NKEYS // pack
    per_chunk = PEER_EB // PEER_NKEYS

    @pl.when(e == 0)
    def _():
        acc_s[...] = jnp.zeros(acc_s.shape, F32)

    def gate_weights(a):
        w = [jnp.zeros((pack, PEER_TM), BF16) for _ in range(n_pack)]
        for h in range(PEER_HEADS):
            n1 = jnp.broadcast_to(n1_ref[h, pl.ds(a, 1), :], (pack, PEER_TM)).astype(BF16)
            e1 = jnp.broadcast_to(e1_ref[h, pl.ds(a, 1), :], (pack, PEER_TM)).astype(BF16)
            for r in range(n_pack):
                rows = slice(r * pack, (r + 1) * pack)
                w[r] = w[r] + jnp.where(r2_ref[h, rows, :] < n1, e2_ref[h, rows, :], 0.0) * e1
        return jnp.concatenate(w, axis=0)

    xnt = xnt_ref[...]
    total = None
    for k in range(PEER_CHUNKS):
        a0 = (e * PEER_CHUNKS + k) * per_chunk
        w = [gate_weights(a0 + aa) for aa in range(per_chunk)]
        act_t = _dot(u_ref[k * PEER_EB:(k + 1) * PEER_EB, :], xnt)
        g = [w[aa] * _gelu_folded(act_t[aa * PEER_NKEYS:(aa + 1) * PEER_NKEYS, :]).astype(BF16)
             for aa in range(per_chunk)]
        part = _dot(vt_ref[:, k * PEER_EB:(k + 1) * PEER_EB], jnp.concatenate(g, axis=0))
        total = part if total is None else total + part
    acc_s[...] += total

    @pl.when(e == pl.num_programs(1) - 1)
    def _():
        o_ref[...] = acc_s[...].T


def _peer_dense(xnt, u_b, vt_b, r2, e2, n1, e1):
    n = xnt.shape[1]
    step = PEER_EB * PEER_CHUNKS
    big = pl.BlockSpec((PEER_HEADS, PEER_NKEYS, PEER_TM), lambda i, e: (0, 0, i))
    return pl.pallas_call(
        _peer_dense_kernel,
        grid=(n // PEER_TM, PEER_EXPERTS // step),
        in_specs=[pl.BlockSpec((D_MODEL, PEER_TM), lambda i, e: (0, i)),
                  pl.BlockSpec((step, D_MODEL), lambda i, e: (e, 0)),
                  pl.BlockSpec((D_MODEL, step), lambda i, e: (0, e)),
                  big, big, big, big],
        out_specs=pl.BlockSpec((PEER_TM, D_MODEL), lambda i, e: (i, 0)),
        out_shape=jax.ShapeDtypeStruct((n, D_MODEL), F32),
        scratch_shapes=[pltpu.VMEM((D_MODEL, PEER_TM), F32)],
        compiler_params=_params(("parallel", "arbitrary")),
        name="peer_dense",
    )(xnt, u_b, vt_b, r2, e2, n1, e1)


def _final_kernel(x1_ref, po_ref, p_ref, gp_ref, wg_ref, wp_ref, gfin_ref, y_ref):
    x2 = x1_ref[...] + po_ref[...]
    gate = _sigmoid(_dot(_rms(x2, gp_ref[...]).astype(BF16), wg_ref[...]))
    x3 = x2 + gate * _dot(p_ref[...].astype(BF16), wp_ref[...])
    y_ref[...] = _rms(x3, gfin_ref[...])


def _final(x1, po, p2, g_ple, w_gate, w_proj, g_final):
    n = x1.shape[0]
    tok = lambda w: pl.BlockSpec((TM, w), lambda i: (i, 0))
    vec = _const_spec((1, D_MODEL))
    return pl.pallas_call(
        _final_kernel,
        grid=(n // TM,),
        in_specs=[tok(D_MODEL), tok(D_MODEL), tok(D_PLE), vec, _const_spec((D_MODEL, D_MODEL)),
                  _const_spec((D_PLE, D_MODEL)), vec],
        out_specs=tok(D_MODEL),
        out_shape=jax.ShapeDtypeStruct((n, D_MODEL), F32),
        compiler_params=_params(("parallel",)),
        name="final",
    )(x1, po, p2, g_ple, w_gate, w_proj, g_final)


def _pad_keys(x, lpad, axis):
    pad = lpad - x.shape[axis]
    if pad == 0:
        return x
    widths = [(0, 0)] * x.ndim
    widths[axis] = (0, pad)
    return jnp.pad(x, widths)


def _layer(x, p, conv_state, h_state, cache_k, cache_v, cache_ki, w):
    b, t, _ = x.shape
    n = b * t
    kvd = N_KV_HEADS * HEAD_DIM
    (rx, rgate, q, k, v, kb, vb, qi, ga, gb, ki, kib, wi) = _inproj(x.reshape(n, D_MODEL), w["g_mix"], w["w_in"])

    hg, conv_new, h_last = _rglru(rx.reshape(b, t, D_RNN), rgate.reshape(b, t, D_RNN), conv_state,
                                  h_state.reshape(b, 1, D_RNN), w["conv_w"], w["conv_b"], w["w_rg_a"],
                                  w["b_rg_a"], w["w_rg_x"], w["b_rg_x"], w["rg_lambda"])

    k_all, v_all, ki_all = kb.reshape(b, t, kvd), vb.reshape(b, t, kvd), kib.reshape(b, t, IDX_DIM)
    past = 0
    if cache_k is not None:
        past = cache_k.shape[1]
        k_all = jnp.concatenate([cache_k.reshape(b, past, kvd).astype(BF16), k_all], axis=1)
        v_all = jnp.concatenate([cache_v.reshape(b, past, kvd).astype(BF16), v_all], axis=1)
        ki_all = jnp.concatenate([cache_ki.astype(BF16), ki_all], axis=1)
    l_real = past + t
    assert past % DSA_BLK == 0 and t % min(t, DSA_BLK) == 0 and n % PEER_TM == 0
    lpad = -(-l_real // (2 * DSA_BLK)) * (2 * DSA_BLK)
    kt = _pad_keys(jnp.swapaxes(k_all, 1, 2), lpad, 2)
    v_all = _pad_keys(v_all, lpad, 1)
    kit = _pad_keys(jnp.swapaxes(ki_all, 1, 2), lpad, 2)
    zeros = jnp.zeros_like(kit)
    kie = jnp.concatenate([kit, zeros], axis=1)
    kio = jnp.concatenate([zeros, kit], axis=1)
    tq = min(t, DSA_BLK)
    attn = _dsa(q.reshape(b, t, D_MODEL), qi.reshape(b, t, IDX_HEADS * IDX_DIM), wi.reshape(b, t, IDX_HEADS),
                kt, v_all, kie, kio, _near_bias(w["rel_bias"], tq),
                past=past, l_real=l_real, n_sel=min(TOPK_MAX, l_real // 4))

    x1, xn, st = _merge(x.reshape(n, D_MODEL), hg.reshape(n, D_RNN), attn.reshape(n, D_MODEL), ga, gb,
                        w["w_a_out"], w["w_b_out"], w["w_o"], w["g_ffn"], w["w_peer_q"], w["peer_sk"])
    r2, e2, n1, e1 = _peer_select(st)
    po = _peer_dense(xn.T, w["peer_u"], w["peer_vt"], r2, e2, n1, e1)
    y = _final(x1, po, p.reshape(n, D_PLE), w["g_ple"], w["w_ple_gate"], w["w_ple_proj"], w["g_final"])
    return (y.reshape(b, t, D_MODEL), k.reshape(b, t, N_KV_HEADS, HEAD_DIM),
            v.reshape(b, t, N_KV_HEADS, HEAD_DIM), ki.reshape(b, t, IDX_DIM), conv_new,
            h_last.reshape(b, D_RNN))


def kernel(x_prompt, x_sample, p_prompt, p_sample, state_conv, state_rglru, cache_k, cache_v, cache_idx_k, rel_bias, g_mix, w_in, conv_w, conv_b, w_rg_a, b_rg_a, w_rg_x, b_rg_x, rg_lambda, w_a_out, w_b_out, w_o, g_ffn, w_peer_q, peer_sub_keys, peer_u, peer_v, g_ple, w_ple_gate, w_ple_proj, g_final):
    assert g_mix.shape[0] == 1, "single trunk layer"
    row = lambda a: a.reshape(1, -1)
    wi_full = w_in[0]
    cut = _C_GA
    tail = IDX_DIM + IDX_HEADS
    w_r = jnp.concatenate([wi_full[:, :cut], wi_full[:, cut + tail:], wi_full[:, cut:cut + tail],
                           jnp.zeros((D_MODEL, _C_END - _C_TAIL - tail), F32)], axis=1).astype(BF16)
    w = dict(
        g_mix=row(g_mix[0]), w_in=w_r, conv_w=conv_w[0], conv_b=row(conv_b[0]),
        w_rg_a=w_rg_a[0].astype(BF16), b_rg_a=row(b_rg_a[0]), w_rg_x=w_rg_x[0].astype(BF16),
        b_rg_x=row(b_rg_x[0]), rg_lambda=row(rg_lambda[0]), rel_bias=rel_bias,
        w_a_out=w_a_out[0].astype(BF16), w_b_out=w_b_out[0].astype(BF16), w_o=w_o[0].astype(BF16),
        g_ffn=row(g_ffn[0]), w_peer_q=w_peer_q[0].astype(BF16),
        peer_sk=peer_sub_keys[0].reshape(2 * PEER_HEADS, PEER_NKEYS, PEER_DK // 2).astype(BF16),
        peer_u=peer_u[0].astype(BF16), peer_vt=peer_v[0].astype(BF16).T,
        g_ple=row(g_ple[0]), w_ple_gate=w_ple_gate[0].astype(BF16), w_ple_proj=w_ple_proj[0].astype(BF16),
        g_final=row(g_final),
    )
    bp = x_prompt.shape[0]
    zc = jnp.zeros((bp, CONV_W - 1, D_RNN), F32)
    zh = jnp.zeros((bp, D_RNN), F32)
    yp, k1, v1, ki1, c1, r1 = _layer(x_prompt, p_prompt[0], zc, zh, None, None, None, w)
    ys, k2, v2, ki2, c2, r2 = _layer(x_sample, p_sample[0], state_conv[0], state_rglru[0],
                                     cache_k[0], cache_v[0], cache_idx_k[0], w)
    return (yp, ys, k1[None], v1[None], ki1[None], c1[None], r1[None],
            k2[None], v2[None], ki2[None], c2[None], r2[None])
```

```python
import functools
import math

import jax
import jax.numpy as jnp
from jax import lax
from jax.experimental import pallas as pl
from jax.experimental.pallas import tpu as pltpu

F32 = jnp.float32
BF16 = jnp.bfloat16
I32 = jnp.int32
LOG2E = math.log2(math.e)

D_MODEL = 1024
CHUNK = 64
CHUNK_SHIFT = 6
D_PLE = 256
D_RNN = 1024
RG_BLOCKS = 8
RG_BLOCK = D_RNN // RG_BLOCKS
CONV_W = 4
RG_C = 8.0
N_HEADS = 8
HEAD_DIM = 128
N_KV_HEADS = 2
KV_GROUP = N_HEADS // N_KV_HEADS
IDX_HEADS = 16
IDX_DIM = 64
TOPK_MAX = 256
NUM_BUCKETS = 32
MAX_DISTANCE = 128
PEER_HEADS = 8
PEER_NKEYS = 128
PEER_EXPERTS = PEER_NKEYS * PEER_NKEYS
PEER_DK = 256
PEER_TOPK = 16
EPS = 1e-6
NEG = -1e30
INT_MIN = -(2 ** 31)

LANES = 128
LANE_SHIFT = 7
VMEM_LIMIT = 56 * 1024 * 1024

_C_RX, _C_RG, _C_Q, _C_K, _C_V, _C_QI, _C_GA, _C_GB, _C_TAIL, _C_END = (
    0, 1024, 2048, 3072, 3328, 3584, 4608, 5632, 6656, 6784)

TM = 256
DSA_BLK = 256
DSA_WIDE = 1024
SEARCH_ROWS = 128
PEER_TM = 512
PEER_EB = 512
PEER_CHUNKS = 4


def _params(sem):
    return pltpu.CompilerParams(dimension_semantics=sem, vmem_limit_bytes=VMEM_LIMIT)


def _const_spec(shape):
    nd = len(shape)
    return pl.BlockSpec(shape, lambda *_: (0,) * nd, pipeline_mode=pl.Buffered(1))


def _rms(x, g):
    return x * lax.rsqrt(jnp.mean(x * x, axis=-1, keepdims=True) + EPS) * g


def _gelu(x):
    return 0.5 * x * (1.0 + jnp.tanh(math.sqrt(2.0 / math.pi) * (x + 0.044715 * (x * x * x))))


def _sigmoid(x):
    return 1.0 / (1.0 + jnp.exp(-x))


def _dot(a, b):
    return jnp.dot(a, b, preferred_element_type=F32)


def _dot_nt(a, b):
    return lax.dot_general(a, b, (((1,), (1,)), ((), ())), preferred_element_type=F32)


def _inproj_kernel(x_ref, g_ref, w_ref, rx_ref, rg_ref, q_ref, k_ref, v_ref, kb_ref, vb_ref,
                   qi_ref, ga_ref, gb_ref, ki_ref, kib_ref, wi_ref):
    n = _rms(x_ref[...], g_ref[...]).astype(BF16)

    def mm(a, b):
        return _dot(n, w_ref[:, a:b])

    rx_ref[...] = mm(_C_RX, _C_RG)
    rg_ref[...] = mm(_C_RG, _C_Q)
    q_ref[...] = mm(_C_Q, _C_K).astype(BF16)
    k = mm(_C_K, _C_V)
    k_ref[...] = k
    kb_ref[...] = k.astype(BF16)
    v = mm(_C_V, _C_QI)
    v_ref[...] = v
    vb_ref[...] = v.astype(BF16)
    qi_ref[...] = mm(_C_QI, _C_GA).astype(BF16)
    ga_ref[...] = mm(_C_GA, _C_GB)
    gb_ref[...] = mm(_C_GB, _C_TAIL)
    tail = mm(_C_TAIL, _C_END)
    ki = tail[:, :IDX_DIM]
    ki_ref[...] = ki
    kib_ref[...] = ki.astype(BF16)
    wi_ref[...] = tail[:, IDX_DIM:IDX_DIM + IDX_HEADS]


def _inproj(x2, g_mix, w_r):
    n = x2.shape[0]
    kvd = N_KV_HEADS * HEAD_DIM
    tok = lambda w: pl.BlockSpec((TM, w), lambda i: (i, 0))
    widths_dtypes = [(D_RNN, F32), (D_RNN, F32), (D_MODEL, BF16), (kvd, F32), (kvd, F32), (kvd, BF16),
                     (kvd, BF16), (IDX_HEADS * IDX_DIM, BF16), (D_MODEL, F32), (D_MODEL, F32),
                     (IDX_DIM, F32), (IDX_DIM, BF16), (IDX_HEADS, F32)]
    return pl.pallas_call(
        _inproj_kernel,
        grid=(n // TM,),
        in_specs=[tok(D_MODEL), _const_spec((1, D_MODEL)), _const_spec(w_r.shape)],
        out_specs=[tok(w) for w, _ in widths_dtypes],
        out_shape=[jax.ShapeDtypeStruct((n, w), dt) for w, dt in widths_dtypes],
        compiler_params=_params(("parallel",)),
        name="inproj",
    )(x2, g_mix, w_r)


def _rglru_kernel(x_ref, gate_ref, cs_ref, h0_ref, cw_ref, cb_ref, wa_ref, ba_ref, wx_ref, bx_ref,
                  lam_ref, hg_ref, cnew_ref, hlast_ref, xp_s, hc_s, *, tt):
    @pl.when(pl.program_id(1) == 0)
    def _():
        xp_s[5:8, :] = cs_ref[0]
        hc_s[...] = h0_ref[0]

    xp_s[8:8 + tt, :] = x_ref[0]
    nl = -lam_ref[...]
    softplus = jnp.maximum(nl, 0.0) + jnp.log1p(jnp.exp(-jnp.abs(nl)))
    row = lax.broadcasted_iota(I32, (tt, RG_BLOCK), 0)
    for n in range(RG_BLOCKS):
        sl = slice(n * RG_BLOCK, (n + 1) * RG_BLOCK)
        xc = cb_ref[:, sl]
        for j in range(CONV_W):
            xc = xc + cw_ref[j:j + 1, sl] * xp_s[5 + j:5 + j + tt, sl]
        xcb = xc.astype(BF16)
        r = _sigmoid(_dot(xcb, wa_ref[n]) + ba_ref[:, sl])
        ig = _sigmoid(_dot(xcb, wx_ref[n]) + bx_ref[:, sl])
        log_a = -RG_C * r * softplus[:, sl]
        a = jnp.exp(log_a)
        mult = jnp.sqrt(-jnp.tanh(log_a) * (a * a + 1.0))
        bv = mult * (ig * xc)
        d = 1
        while d < tt:
            keep = row >= d
            a_sh = pltpu.roll(a, d, 0)
            b_sh = pltpu.roll(bv, d, 0)
            bv = jnp.where(keep, a * b_sh + bv, bv)
            a = jnp.where(keep, a * a_sh, a)
            d *= 2
        h = a * hc_s[:, sl] + bv
        hc_s[:, sl] = h[tt - 1:tt, :]
        hg_ref[0, :, sl] = (h * _gelu(gate_ref[0, :, sl])).astype(BF16)
    tail = xp_s[tt + 5:tt + 8, :]
    cnew_ref[0] = tail
    xp_s[5:8, :] = tail
    hlast_ref[0] = hc_s[...]


def _rglru(rx, rgate, conv_state, h0, conv_w, conv_b, w_a, b_a, w_x, b_x, lam):
    b, t, _ = rx.shape
    tt = min(t, 256)
    seq = pl.BlockSpec((1, tt, D_RNN), lambda i, j: (i, j, 0))
    per_b = lambda r: pl.BlockSpec((1, r, D_RNN), lambda i, j: (i, 0, 0))
    vec = _const_spec((1, D_RNN))
    wblk = _const_spec((RG_BLOCKS, RG_BLOCK, RG_BLOCK))
    return pl.pallas_call(
        functools.partial(_rglru_kernel, tt=tt),
        grid=(b, t // tt),
        in_specs=[seq, seq, per_b(CONV_W - 1), per_b(1), _const_spec((CONV_W, D_RNN)), vec,
                  wblk, vec, wblk, vec, vec],
        out_specs=[seq, per_b(CONV_W - 1), per_b(1)],
        out_shape=[jax.ShapeDtypeStruct((b, t, D_RNN), BF16),
                   jax.ShapeDtypeStruct((b, CONV_W - 1, D_RNN), F32),
                   jax.ShapeDtypeStruct((b, 1, D_RNN), F32)],
        scratch_shapes=[pltpu.VMEM((tt + 8, D_RNN), F32), pltpu.VMEM((1, D_RNN), F32)],
        compiler_params=_params(("parallel", "arbitrary")),
        name="rglru",
    )(rx, rgate, conv_state, h0, conv_w, conv_b, w_a, b_a, w_x, b_x, lam)


def _dsa_kernel(q_ref, qi_ref, wi_ref, kt_ref, v_ref, kie_ref, kio_ref, nb_ref, o_ref,
                key_s, wb_s, qs_s, m_s, l_s, acc_s, *, tq, past, l_real, n_sel):
    kb = DSA_BLK
    nc = kb // LANES
    i = pl.program_id(1)
    q0 = past + i * tq
    own = q0 // kb
    nkb = own + 1
    n_unit = (nkb + 1) // 2

    def lanes_at(k0):
        return lax.shift_right_logical(k0, LANE_SHIFT)

    wi = wi_ref[0] * (IDX_HEADS ** -0.5 * IDX_DIM ** -0.5)
    for h in range(IDX_HEADS):
        wb_s[h] = jnp.broadcast_to(wi[:, h:h + 1], (tq, LANES))
    q_chunk = (q0 + lax.broadcasted_iota(I32, (tq, LANES), 0)) >> CHUNK_SHIFT
    lane = lax.broadcasted_iota(I32, (tq, LANES), 1)

    def score_block(j, carry):
        k0 = pl.multiple_of(j * kb, kb)
        acc = [jnp.zeros((tq, LANES), F32) for _ in range(nc)]
        for h2 in range(IDX_HEADS // 2):
            qpair = qi_ref[0, :, h2 * LANES:(h2 + 1) * LANES]
            for par, kref in ((0, kie_ref), (1, kio_ref)):
                s = _dot(qpair, kref[0, :, pl.ds(k0, kb)])
                w = wb_s[2 * h2 + par]
                for c in range(nc):
                    acc[c] = acc[c] + jnp.maximum(s[:, c * LANES:(c + 1) * LANES], 0.0) * w
        for c in range(nc):
            bits = pltpu.bitcast(acc[c], I32)
            key = bits ^ ((bits >> 31) & 0x7FFFFFFF)
            kpos = k0 + c * LANES + lane
            key = jnp.where((kpos >> CHUNK_SHIFT) <= q_chunk, key, INT_MIN)
            key = jnp.where(kpos < l_real, key, INT_MIN)
            key_s[lanes_at(k0 + c * LANES)] = key
        return carry

    lax.fori_loop(0, nkb, score_block, 0)

    @pl.when(nkb % 2 == 1)
    def _():
        for c in range(nc):
            key_s[lanes_at(nkb * kb + c * LANES)] = jnp.full((tq, LANES), INT_MIN, I32)

    unit = 2 * kb
    strip = min(tq, SEARCH_ROWS)
    n_strip = tq // strip

    def count_ge(cands):
        totals = []
        for s in range(n_strip):
            rows = slice(s * strip, (s + 1) * strip)
            cand = cands[s]

            def body(u, cnt, rows=rows, cand=cand):
                for c in range(unit // LANES):
                    blk = key_s[lanes_at(u * unit + c * LANES), rows, :]
                    cnt = cnt + jnp.where(blk >= cand, 1.0, 0.0)
                return cnt

            cnt = lax.fori_loop(0, n_unit, body, jnp.zeros((strip, LANES), F32))
            totals.append(jnp.broadcast_to(jnp.sum(cnt, axis=1, keepdims=True), (strip, LANES)))
        return totals

    zero = jnp.zeros((strip, LANES), I32)
    res = tuple(jnp.where(t >= n_sel, zero, INT_MIN) for t in count_ge([zero] * n_strip))

    def bit_step(it, res):
        cands = [r | (1 << (30 - it)) for r in res]
        return tuple(jnp.where(t >= n_sel, c, r) for t, c, r in zip(count_ge(cands), cands, res))

    res = lax.fori_loop(0, 31, bit_step, res)
    thr = jnp.maximum(jnp.concatenate(res, axis=0), INT_MIN + 1)

    for g in range(N_KV_HEADS):
        for hh in range(KV_GROUP):
            h = g * KV_GROUP + hh
            qs_s[g, hh * tq:(hh + 1) * tq, :] = q_ref[0, :, h * HEAD_DIM:(h + 1) * HEAD_DIM]
    m_s[...] = jnp.full(m_s.shape, NEG, F32)
    l_s[...] = jnp.zeros(l_s.shape, F32)
    acc_s[...] = jnp.zeros(acc_s.shape, F32)
    scale2 = HEAD_DIM ** -0.5 * LOG2E

    def attend(k0, width, near):
        ncw = width // LANES
        mb = [jnp.where(key_s[lanes_at(k0 + c * LANES)] >= thr, 0.0, NEG) for c in range(ncw)]
        for g in range(N_KV_HEADS):
            s_all = _dot(qs_s[g], kt_ref[0, g * HEAD_DIM:(g + 1) * HEAD_DIM, pl.ds(k0, width)]) * scale2
            vb = v_ref[0, pl.ds(k0, width), g * HEAD_DIM:(g + 1) * HEAD_DIM]
            for hh in range(KV_GROUP):
                h = g * KV_GROUP + hh
                s = s_all[hh * tq:(hh + 1) * tq, :]
                sc = []
                for c in range(ncw):
                    x = s[:, c * LANES:(c + 1) * LANES] + mb[c]
                    if near is not None:
                        x = x + nb_ref[near, h, :, c * LANES:(c + 1) * LANES]
                    sc.append(x)
                mx = sc[0]
                for c in range(1, ncw):
                    mx = jnp.maximum(mx, sc[c])
                m_old = m_s[h]
                m_new = jnp.maximum(m_old, jnp.broadcast_to(jnp.max(mx, axis=1, keepdims=True), (tq, LANES)))
                alpha = jnp.exp2(m_old - m_new)
                p = [jnp.exp2(x - m_new) for x in sc]
                ps = p[0]
                for c in range(1, ncw):
                    ps = ps + p[c]
                l_s[h] = alpha * l_s[h] + jnp.broadcast_to(jnp.sum(ps, axis=1, keepdims=True), (tq, LANES))
                m_s[h] = m_new
                pb = jnp.concatenate(p, axis=1).astype(BF16)
                acc_s[h] = alpha * acc_s[h] + _dot(pb, vb)

    n_far = jnp.maximum(own - 1, 0)
    per_wide = DSA_WIDE // kb
    n_wide = n_far // per_wide

    def wide_step(j, carry):
        attend(pl.multiple_of(j * DSA_WIDE, DSA_WIDE), DSA_WIDE, None)
        return carry

    def far_block(j, carry):
        attend(pl.multiple_of(j * kb, kb), kb, None)
        return carry

    lax.fori_loop(0, n_wide, wide_step, 0)
    lax.fori_loop(n_wide * per_wide, n_far, far_block, 0)

    @pl.when(own >= 1)
    def _():
        attend(pl.multiple_of((own - 1) * kb, kb), kb, 0)

    attend(pl.multiple_of(own * kb, kb), kb, 1)

    for h in range(N_HEADS):
        o_ref[0, :, h * HEAD_DIM:(h + 1) * HEAD_DIM] = (acc_s[h] / l_s[h]).astype(BF16)


def _t5_bucket(rel):
    nb = NUM_BUCKETS // 2
    max_exact = nb // 2
    ret = jnp.where(rel > 0, nb, 0)
    n = jnp.abs(rel)
    n_f = jnp.maximum(n, 1).astype(F32)
    large = max_exact + (jnp.log(n_f / max_exact) / math.log(MAX_DISTANCE / max_exact)
                         * (nb - max_exact)).astype(I32)
    large = jnp.minimum(large, nb - 1)
    return ret + jnp.where(n < max_exact, n, large)


def _near_bias(rel_bias, tq):
    a = jnp.arange(tq, dtype=I32)[:, None]
    b = jnp.arange(DSA_BLK, dtype=I32)[None, :]
    centered = (rel_bias - rel_bias[NUM_BUCKETS // 2 - 1]) * LOG2E
    tiles = []
    for d in (-DSA_BLK, 0):
        bucket = jnp.where(d + b - a > -MAX_DISTANCE, _t5_bucket(d + b - a), NUM_BUCKETS // 2 - 1)
        t = jnp.zeros((N_HEADS, tq, DSA_BLK), F32)
        for c in range(NUM_BUCKETS):
            t = jnp.where((bucket == c)[None], centered[c][:, None, None], t)
        tiles.append(t)
    return jnp.stack(tiles)


def _dsa(q, qi, wi, kt, v, kie, kio, nbias, *, past, l_real, n_sel):
    b, t, _ = q.shape
    tq = min(t, DSA_BLK)
    lpad = kt.shape[2]
    tile = lambda w: pl.BlockSpec((1, tq, w), lambda i, j: (i, j, 0))
    per_b = lambda r, c: pl.BlockSpec((1, r, c), lambda i, j: (i, 0, 0), pipeline_mode=pl.Buffered(1))
    kvd = N_KV_HEADS * HEAD_DIM
    return pl.pallas_call(
        functools.partial(_dsa_kernel, tq=tq, past=past, l_real=l_real, n_sel=n_sel),
        grid=(b, t // tq),
        in_specs=[tile(D_MODEL), tile(IDX_HEADS * IDX_DIM), tile(IDX_HEADS),
                  per_b(kvd, lpad), per_b(lpad, kvd), per_b(LANES, lpad), per_b(LANES, lpad),
                  _const_spec(nbias.shape)],
        out_specs=tile(D_MODEL),
        out_shape=jax.ShapeDtypeStruct((b, t, D_MODEL), BF16),
        scratch_shapes=[pltpu.VMEM((lpad // LANES, tq, LANES), I32),
                        pltpu.VMEM((IDX_HEADS, tq, LANES), F32),
                        pltpu.VMEM((N_KV_HEADS, KV_GROUP * tq, HEAD_DIM), BF16),
                        pltpu.VMEM((N_HEADS, tq, LANES), F32),
                        pltpu.VMEM((N_HEADS, tq, LANES), F32),
                        pltpu.VMEM((N_HEADS, tq, HEAD_DIM), F32)],
        compiler_params=_params(("parallel", "arbitrary")),
        name="dsa",
    )(q, qi, wi, kt, v, kie, kio, nbias)


def _merge_kernel(x_ref, hg_ref, at_ref, ga_ref, gb_ref, wa_ref, wb_ref, wo_ref, gf_ref, wq_ref, sk_ref,
                  x1_ref, xn_ref, st_ref):
    ya = _dot(hg_ref[...], wa_ref[...])
    yb = _dot(at_ref[...], wb_ref[...])
    m = _sigmoid(ga_ref[...]) * ya + _sigmoid(gb_ref[...]) * yb
    x1 = x_ref[...] + _dot(m.astype(BF16), wo_ref[...])
    x1_ref[...] = x1
    xn = _rms(x1, gf_ref[...]).astype(BF16)
    xn_ref[...] = xn
    qp = _dot(xn, wq_ref[...])
    for j in range(2 * PEER_HEADS):
        qj = qp[:, j * LANES:(j + 1) * LANES].astype(BF16)
        st_ref[j] = _dot_nt(sk_ref[j], qj)


def _merge(x2, hg, attn, ga, gb, w_a_out, w_b_out, w_o, g_ffn, w_q, sk):
    n = x2.shape[0]
    tok = lambda w: pl.BlockSpec((TM, w), lambda i: (i, 0))
    sq = _const_spec((D_MODEL, D_MODEL))
    nsk = 2 * PEER_HEADS
    return pl.pallas_call(
        _merge_kernel,
        grid=(n // TM,),
        in_specs=[tok(D_MODEL)] * 5 + [sq, sq, sq, _const_spec((1, D_MODEL)),
                                       _const_spec(w_q.shape), _const_spec(sk.shape)],
        out_specs=[tok(D_MODEL), tok(D_MODEL), pl.BlockSpec((nsk, PEER_NKEYS, TM), lambda i: (0, 0, i))],
        out_shape=[jax.ShapeDtypeStruct((n, D_MODEL), F32), jax.ShapeDtypeStruct((n, D_MODEL), BF16),
                   jax.ShapeDtypeStruct((nsk, PEER_NKEYS, n), F32)],
        compiler_params=_params(("parallel",)),
        name="merge",
    )(x2, hg, attn, ga, gb, w_a_out, w_b_out, w_o, g_ffn, w_q, sk)


_PAIR_LIMIT = [PEER_TOPK // (i + 1) for i in range(PEER_TOPK)]
_NOT_TOP = 99.0


def _peer_select_kernel(st_ref, r2_ref, e2_ref, n1_ref, e1_ref):
    ninf = -jnp.inf
    sub = 8
    row8 = lax.broadcasted_iota(I32, (sub, PEER_TM), 0)

    def top(s):
        vals, cur = [], s
        rank = jnp.full(s.shape, _NOT_TOP, F32)
        for k in range(PEER_TOPK):
            m = jnp.max(cur, axis=0, keepdims=True)
            vals.append(m)
            sel = cur == m
            rank = jnp.where(sel, float(k), rank)
            cur = jnp.where(sel, ninf, cur)
        return vals, rank

    def head(h, carry):
        s1 = st_ref[2 * h]
        s2 = st_ref[2 * h + 1]
        v1, r1 = top(s1)
        v2, r2 = top(s2)
        v2t = jnp.concatenate(v2, axis=0)
        v1_low = jnp.concatenate(v1[sub:], axis=0)
        groups = [v1[0] + v2t]
        for i in range(1, sub):
            groups.append(jnp.where(row8 < _PAIR_LIMIT[i], v1[i] + v2t[:sub], ninf))
        groups.append(v1_low + v2[0])
        cand = jnp.concatenate(groups, axis=0)
        cur = cand
        for r in range(PEER_TOPK):
            thr = jnp.max(cur, axis=0, keepdims=True)
            if r + 1 < PEER_TOPK:
                cur = jnp.where(cur == thr, ninf, cur)
        picked = cand >= thr
        z = jnp.sum(jnp.where(picked, jnp.exp(cand - (v1[0] + v2[0])), 0.0), axis=0, keepdims=True)
        cnt = jnp.where(picked, 1.0, 0.0)
        n_of_rank = [jnp.sum(cnt[:PEER_TOPK], axis=0, keepdims=True)]
        for i in range(1, sub):
            lo = PEER_TOPK + (i - 1) * sub
            n_of_rank.append(jnp.sum(cnt[lo:lo + sub], axis=0, keepdims=True))
        lo = PEER_TOPK + (sub - 1) * sub
        for i in range(sub, PEER_TOPK):
            n_of_rank.append(cnt[lo + i - sub:lo + i - sub + 1])
        n1 = jnp.zeros(s1.shape, F32)
        for i in range(PEER_TOPK):
            n1 = jnp.where(r1 == float(i), n_of_rank[i], n1)
        r2_ref[h] = r2.astype(BF16)
        e2_ref[h] = (jnp.exp(s2 - v2[0]) / z).astype(BF16)
        n1_ref[h] = n1
        e1_ref[h] = jnp.exp(s1 - v1[0])
        return carry

    lax.fori_loop(0, PEER_HEADS, head, 0)


def _peer_select(st):
    n = st.shape[2]
    big = pl.BlockSpec((PEER_HEADS, PEER_NKEYS, PEER_TM), lambda i: (0, 0, i))
    sds = lambda dt: jax.ShapeDtypeStruct((PEER_HEADS, PEER_NKEYS, n), dt)
    return pl.pallas_call(
        _peer_select_kernel,
        grid=(n // PEER_TM,),
        in_specs=[pl.BlockSpec((2 * PEER_HEADS, PEER_NKEYS, PEER_TM), lambda i: (0, 0, i))],
        out_specs=[big, big, big, big],
        out_shape=[sds(BF16), sds(BF16), sds(F32), sds(F32)],
        compiler_params=_params(("parallel",)),
        name="peer_select",
    )(st)


def _gelu_folded(x):
    c = math.sqrt(2.0 / math.pi)
    half = 0.5 * x
    return half + half * jnp.tanh(x * (c + (c * 0.044715) * (x * x)))


def _peer_dense_kernel(xnt_ref, u_ref, vt_ref, r2_ref, e2_ref, n1_ref, e1_ref, o_ref, acc_s):
    e = pl.program_id(1)
    pack = 16
    n_pack = PEER_NKEYS // pack
    per_chunk = PEER_EB // PEER_NKEYS

    @pl.when(e == 0)
    def _():
        acc_s[...] = jnp.zeros(acc_s.shape, F32)

    def gate_weights(a):
        w = [jnp.zeros((pack, PEER_TM), BF16) for _ in range(n_pack)]
        for h in range(PEER_HEADS):
            n1 = jnp.broadcast_to(n1_ref[h, pl.ds(a, 1), :], (pack, PEER_TM)).astype(BF16)
            e1 = jnp.broadcast_to(e1_ref[h, pl.ds(a, 1), :], (pack, PEER_TM)).astype(BF16)
            for r in range(n_pack):
                rows = slice(r * pack, (r + 1) * pack)
                w[r] = w[r] + jnp.where(r2_ref[h, rows, :] < n1, e2_ref[h, rows, :], 0.0) * e1
        return jnp.concatenate(w, axis=0)

    xnt = xnt_ref[...]
    total = None
    for k in range(PEER_CHUNKS):
        a0 = (e * PEER_CHUNKS + k) * per_chunk
        w = [gate_weights(a0 + aa) for aa in range(per_chunk)]
        act_t = _dot(u_ref[k * PEER_EB:(k + 1) * PEER_EB, :], xnt)
        g = [w[aa] * _gelu_folded(act_t[aa * PEER_NKEYS:(aa + 1) * PEER_NKEYS, :]).astype(BF16)
             for aa in range(per_chunk)]
        part = _dot(vt_ref[:, k * PEER_EB:(k + 1) * PEER_EB], jnp.concatenate(g, axis=0))
        total = part if total is None else total + part
    acc_s[...] += total

    @pl.when(e == pl.num_programs(1) - 1)
    def _():
        o_ref[...] = acc_s[...].T


def _peer_dense(xnt, u_b, vt_b, r2, e2, n1, e1):
    n = xnt.shape[1]
    step = PEER_EB * PEER_CHUNKS
    big = pl.BlockSpec((PEER_HEADS, PEER_NKEYS, PEER_TM), lambda i, e: (0, 0, i))
    return pl.pallas_call(
        _peer_dense_kernel,
        grid=(n // PEER_TM, PEER_EXPERTS // step),
        in_specs=[pl.BlockSpec((D_MODEL, PEER_TM), lambda i, e: (0, i)),
                  pl.BlockSpec((step, D_MODEL), lambda i, e: (e, 0)),
                  pl.BlockSpec((D_MODEL, step), lambda i, e: (0, e)),
                  big, big, big, big],
        out_specs=pl.BlockSpec((PEER_TM, D_MODEL), lambda i, e: (i, 0)),
        out_shape=jax.ShapeDtypeStruct((n, D_MODEL), F32),
        scratch_shapes=[pltpu.VMEM((D_MODEL, PEER_TM), F32)],
        compiler_params=_params(("parallel", "arbitrary")),
        name="peer_dense",
    )(xnt, u_b, vt_b, r2, e2, n1, e1)


def _final_kernel(x1_ref, po_ref, p_ref, gp_ref, wg_ref, wp_ref, gfin_ref, y_ref):
    x2 = x1_ref[...] + po_ref[...]
    gate = _sigmoid(_dot(_rms(x2, gp_ref[...]).astype(BF16), wg_ref[...]))
    x3 = x2 + gate * _dot(p_ref[...].astype(BF16), wp_ref[...])
    y_ref[...] = _rms(x3, gfin_ref[...])


def _final(x1, po, p2, g_ple, w_gate, w_proj, g_final):
    n = x1.shape[0]
    tok = lambda w: pl.BlockSpec((TM, w), lambda i: (i, 0))
    vec = _const_spec((1, D_MODEL))
    return pl.pallas_call(
        _final_kernel,
        grid=(n // TM,),
        in_specs=[tok(D_MODEL), tok(D_MODEL), tok(D_PLE), vec, _const_spec((D_MODEL, D_MODEL)),
                  _const_spec((D_PLE, D_MODEL)), vec],
        out_specs=tok(D_MODEL),
        out_shape=jax.ShapeDtypeStruct((n, D_MODEL), F32),
        compiler_params=_params(("parallel",)),
        name="final",
    )(x1, po, p2, g_ple, w_gate, w_proj, g_final)


def _pad_keys(x, lpad, axis):
    pad = lpad - x.shape[axis]
    if pad == 0:
        return x
    widths = [(0, 0)] * x.ndim
    widths[axis] = (0, pad)
    return jnp.pad(x, widths)


def _layer(x, p, conv_state, h_state, cache_k, cache_v, cache_ki, w):
    b, t, _ = x.shape
    n = b * t
    kvd = N_KV_HEADS * HEAD_DIM
    (rx, rgate, q, k, v, kb, vb, qi, ga, gb, ki, kib, wi) = _inproj(x.reshape(n, D_MODEL), w["g_mix"], w["w_in"])

    hg, conv_new, h_last = _rglru(rx.reshape(b, t, D_RNN), rgate.reshape(b, t, D_RNN), conv_state,
                                  h_state.reshape(b, 1, D_RNN), w["conv_w"], w["conv_b"], w["w_rg_a"],
                                  w["b_rg_a"], w["w_rg_x"], w["b_rg_x"], w["rg_lambda"])

    k_all, v_all, ki_all = kb.reshape(b, t, kvd), vb.reshape(b, t, kvd), kib.reshape(b, t, IDX_DIM)
    past = 0
    if cache_k is not None:
        past = cache_k.shape[1]
        k_all = jnp.concatenate([cache_k.reshape(b, past, kvd).astype(BF16), k_all], axis=1)
        v_all = jnp.concatenate([cache_v.reshape(b, past, kvd).astype(BF16), v_all], axis=1)
        ki_all = jnp.concatenate([cache_ki.astype(BF16), ki_all], axis=1)
    l_real = past + t
    assert past % DSA_BLK == 0 and t % min(t, DSA_BLK) == 0 and n % PEER_TM == 0
    lpad = -(-l_real // (2 * DSA_BLK)) * (2 * DSA_BLK)
    kt = _pad_keys(jnp.swapaxes(k_all, 1, 2), lpad, 2)
    v_all = _pad_keys(v_all, lpad, 1)
    kit = _pad_keys(jnp.swapaxes(ki_all, 1, 2), lpad, 2)
    zeros = jnp.zeros_like(kit)
    kie = jnp.concatenate([kit, zeros], axis=1)
    kio = jnp.concatenate([zeros, kit], axis=1)
    tq = min(t, DSA_BLK)
    attn = _dsa(q.reshape(b, t, D_MODEL), qi.reshape(b, t, IDX_HEADS * IDX_DIM), wi.reshape(b, t, IDX_HEADS),
                kt, v_all, kie, kio, _near_bias(w["rel_bias"], tq),
                past=past, l_real=l_real, n_sel=min(TOPK_MAX, l_real // 4))
    x1, xn, st = _merge(x.reshape(n, D_MODEL), hg.reshape(n, D_RNN), attn.reshape(n, D_MODEL), ga, gb,
                        w["w_a_out"], w["w_b_out"], w["w_o"], w["g_ffn"], w["w_peer_q"], w["peer_sk"])
    r2, e2, n1, e1 = _peer_select(st)
    po = _peer_dense(xn.T, w["peer_u"], w["peer_vt"], r2, e2, n1, e1)
    y = _final(x1, po, p.reshape(n, D_PLE), w["g_ple"], w["w_ple_gate"], w["w_ple_proj"], w["g_final"])
    return (y.reshape(b, t, D_MODEL), k.reshape(b, t, N_KV_HEADS, HEAD_DIM),
            v.reshape(b, t, N_KV_HEADS, HEAD_DIM), ki.reshape(b, t, IDX_DIM), conv_new,
            h_last.reshape(b, D_RNN))


def kernel(x_prompt, x_sample, p_prompt, p_sample, state_conv, state_rglru, cache_k, cache_v, cache_idx_k, rel_bias, g_mix, w_in, conv_w, conv_b, w_rg_a, b_rg_a, w_rg_x, b_rg_x, rg_lambda, w_a_out, w_b_out, w_o, g_ffn, w_peer_q, peer_sub_keys, peer_u, peer_v, g_ple, w_ple_gate, w_ple_proj, g_final):
    assert g_mix.shape[0] == 1, "single trunk layer"
    row = lambda a: a.reshape(1, -1)
    wi_full = w_in[0]
    cut = _C_GA
    tail = IDX_DIM + IDX_HEADS
    w_r = jnp.concatenate([wi_full[:, :cut], wi_full[:, cut + tail:], wi_full[:, cut:cut + tail],
                           jnp.zeros((D_MODEL, _C_END - _C_TAIL - tail), F32)], axis=1).astype(BF16)
    w = dict(
        g_mix=row(g_mix[0]), w_in=w_r, conv_w=conv_w[0], conv_b=row(conv_b[0]),
        w_rg_a=w_rg_a[0].astype(BF16), b_rg_a=row(b_rg_a[0]), w_rg_x=w_rg_x[0].astype(BF16),
        b_rg_x=row(b_rg_x[0]), rg_lambda=row(rg_lambda[0]), rel_bias=rel_bias,
        w_a_out=w_a_out[0].astype(BF16), w_b_out=w_b_out[0].astype(BF16), w_o=w_o[0].astype(BF16),
        g_ffn=row(g_ffn[0]), w_peer_q=w_peer_q[0].astype(BF16),
        peer_sk=peer_sub_keys[0].reshape(2 * PEER_HEADS, PEER_NKEYS, PEER_DK // 2).astype(BF16),
        peer_u=peer_u[0].astype(BF16), peer_vt=peer_v[0].astype(BF16).T,
        g_ple=row(g_ple[0]), w_ple_gate=w_ple_gate[0].astype(BF16), w_ple_proj=w_ple_proj[0].astype(BF16),
        g_final=row(g_final),
    )
    bp = x_prompt.shape[0]
    zc = jnp.zeros((bp, CONV_W - 1, D_RNN), F32)
    zh = jnp.zeros((bp, D_RNN), F32)
    yp, k1, v1, ki1, c1, r1 = _layer(x_prompt, p_prompt[0], zc, zh, None, None, None, w)
    ys, k2, v2, ki2, c2, r2 = _layer(x_sample, p_sample[0], state_conv[0], state_rglru[0],
                                     cache_k[0], cache_v[0], cache_idx_k[0], w)
    return (yp, ys, k1[None], v1[None], ki1[None], c1[None], r1[None],
            k2[None], v2[None], ki2[None], c2[None], r2[None])
```

```python
import functools
import math

import jax
import jax.numpy as jnp
from jax import lax
from jax.experimental import pallas as pl
from jax.experimental.pallas import tpu as pltpu

F32 = jnp.float32
BF16 = jnp.bfloat16
I32 = jnp.int32
LOG2E = math.log2(math.e)

D_MODEL = 1024
CHUNK = 64
CHUNK_SHIFT = 6
D_PLE = 256
D_RNN = 1024
RG_BLOCKS = 8
RG_BLOCK = D_RNN // RG_BLOCKS
CONV_W = 4
RG_C = 8.0
N_HEADS = 8
HEAD_DIM = 128
N_KV_HEADS = 2
KV_GROUP = N_HEADS // N_KV_HEADS
IDX_HEADS = 16
IDX_DIM = 64
TOPK_MAX = 256
NUM_BUCKETS = 32
MAX_DISTANCE = 128
PEER_HEADS = 8
PEER_NKEYS = 128
PEER_EXPERTS = PEER_NKEYS * PEER_NKEYS
PEER_DK = 256
PEER_TOPK = 16
EPS = 1e-6
NEG = -1e30
INT_MIN = -(2 ** 31)

LANES = 128
LANE_SHIFT = 7
VMEM_LIMIT = 56 * 1024 * 1024

_C_RX, _C_RG, _C_Q, _C_K, _C_V, _C_QI, _C_GA, _C_GB, _C_TAIL, _C_END = (
    0, 1024, 2048, 3072, 3328, 3584, 4608, 5632, 6656, 6784)

TM = 256
DSA_BLK = 256
DSA_WIDE = 1024
SEARCH_ROWS = 128
PEER_TM = 512
PEER_EB = 512
PEER_CHUNKS = 4


def _params(sem):
    return pltpu.CompilerParams(dimension_semantics=sem, vmem_limit_bytes=VMEM_LIMIT)


def _const_spec(shape):
    nd = len(shape)
    return pl.BlockSpec(shape, lambda *_: (0,) * nd, pipeline_mode=pl.Buffered(1))


def _rms(x, g):
    return x * lax.rsqrt(jnp.mean(x * x, axis=-1, keepdims=True) + EPS) * g


def _gelu(x):
    return 0.5 * x * (1.0 + jnp.tanh(math.sqrt(2.0 / math.pi) * (x + 0.044715 * (x * x * x))))


def _sigmoid(x):
    return 1.0 / (1.0 + jnp.exp(-x))


def _dot(a, b):
    return jnp.dot(a, b, preferred_element_type=F32)


def _dot_nt(a, b):
    return lax.dot_general(a, b, (((1,), (1,)), ((), ())), preferred_element_type=F32)


def _inproj_kernel(x_ref, g_ref, w_ref, rx_ref, rg_ref, q_ref, k_ref, v_ref, kb_ref, vb_ref,
                   qi_ref, ga_ref, gb_ref, ki_ref, kib_ref, wi_ref, *key_major_refs):
    n = _rms(x_ref[...], g_ref[...]).astype(BF16)

    def mm(a, b):
        return _dot(n, w_ref[:, a:b])

    rx_ref[...] = mm(_C_RX, _C_RG)
    rg_ref[...] = mm(_C_RG, _C_Q)
    q_ref[...] = mm(_C_Q, _C_K).astype(BF16)
    k = mm(_C_K, _C_V)
    k_ref[...] = k
    kb_ref[...] = k.astype(BF16)
    v = mm(_C_V, _C_QI)
    v_ref[...] = v
    vb_ref[...] = v.astype(BF16)
    qi_ref[...] = mm(_C_QI, _C_GA).astype(BF16)
    ga_ref[...] = mm(_C_GA, _C_GB)
    gb_ref[...] = mm(_C_GB, _C_TAIL)
    tail = mm(_C_TAIL, _C_END)
    ki = tail[:, :IDX_DIM]
    ki_ref[...] = ki
    kib_ref[...] = ki.astype(BF16)
    wi_ref[...] = tail[:, IDX_DIM:IDX_DIM + IDX_HEADS]
    if key_major_refs:
        kt_ref, kie_ref, kio_ref = key_major_refs
        kt_ref[0] = k.T.astype(BF16)
        tail_t = tail.T
        kit = tail_t[:IDX_DIM].astype(BF16)
        zeros = jnp.zeros_like(kit)
        kie_ref[0] = jnp.concatenate([kit, zeros], axis=0)
        kio_ref[0] = jnp.concatenate([zeros, kit], axis=0)


def _inproj(x2, g_mix, w_r, seq_len):
    n = x2.shape[0]
    kvd = N_KV_HEADS * HEAD_DIM
    tok = lambda w: pl.BlockSpec((TM, w), lambda i: (i, 0))
    extra_specs, extra_shapes = [], []
    if seq_len % TM == 0:
        per_seq = seq_len // TM
        kmaj = lambda r: pl.BlockSpec((1, r, TM), lambda i: (i // per_seq, 0, i % per_seq))
        extra_specs = [kmaj(kvd), kmaj(LANES), kmaj(LANES)]
        extra_shapes = [jax.ShapeDtypeStruct((n // seq_len, r, seq_len), BF16) for r in (kvd, LANES, LANES)]
    widths_dtypes = [(D_RNN, F32), (D_RNN, F32), (D_MODEL, BF16), (kvd, F32), (kvd, F32), (kvd, BF16),
                     (kvd, BF16), (IDX_HEADS * IDX_DIM, BF16), (D_MODEL, F32), (D_MODEL, F32),
                     (IDX_DIM, F32), (IDX_DIM, BF16), (IDX_HEADS, F32)]
    return pl.pallas_call(
        _inproj_kernel,
        grid=(n // TM,),
        in_specs=[tok(D_MODEL), _const_spec((1, D_MODEL)), _const_spec(w_r.shape)],
        out_specs=[tok(w) for w, _ in widths_dtypes] + extra_specs,
        out_shape=[jax.ShapeDtypeStruct((n, w), dt) for w, dt in widths_dtypes] + extra_shapes,
        compiler_params=_params(("parallel",)),
        name="inproj",
    )(x2, g_mix, w_r)


def _rglru_kernel(x_ref, gate_ref, cs_ref, h0_ref, cw_ref, cb_ref, wa_ref, ba_ref, wx_ref, bx_ref,
                  lam_ref, hg_ref, cnew_ref, hlast_ref, xp_s, hc_s, *, tt):
    @pl.when(pl.program_id(1) == 0)
    def _():
        xp_s[5:8, :] = cs_ref[0]
        hc_s[...] = h0_ref[0]

    xp_s[8:8 + tt, :] = x_ref[0]
    nl = -lam_ref[...]
    softplus = jnp.maximum(nl, 0.0) + jnp.log1p(jnp.exp(-jnp.abs(nl)))
    row = lax.broadcasted_iota(I32, (tt, RG_BLOCK), 0)
    for n in range(RG_BLOCKS):
        sl = slice(n * RG_BLOCK, (n + 1) * RG_BLOCK)
        xc = cb_ref[:, sl]
        for j in range(CONV_W):
            xc = xc + cw_ref[j:j + 1, sl] * xp_s[5 + j:5 + j + tt, sl]
        xcb = xc.astype(BF16)
        r = _sigmoid(_dot(xcb, wa_ref[n]) + ba_ref[:, sl])
        ig = _sigmoid(_dot(xcb, wx_ref[n]) + bx_ref[:, sl])
        log_a = -RG_C * r * softplus[:, sl]
        a = jnp.exp(log_a)
        mult = jnp.sqrt(-jnp.tanh(log_a) * (a * a + 1.0))
        bv = mult * (ig * xc)
        d = 1
        while d < tt:
            keep = row >= d
            a_sh = pltpu.roll(a, d, 0)
            b_sh = pltpu.roll(bv, d, 0)
            bv = jnp.where(keep, a * b_sh + bv, bv)
            a = jnp.where(keep, a * a_sh, a)
            d *= 2
        h = a * hc_s[:, sl] + bv
        hc_s[:, sl] = h[tt - 1:tt, :]
        hg_ref[0, :, sl] = (h * _gelu(gate_ref[0, :, sl])).astype(BF16)
    tail = xp_s[tt + 5:tt + 8, :]
    cnew_ref[0] = tail
    xp_s[5:8, :] = tail
    hlast_ref[0] = hc_s[...]


def _rglru(rx, rgate, conv_state, h0, conv_w, conv_b, w_a, b_a, w_x, b_x, lam):
    b, t, _ = rx.shape
    tt = min(t, 256)
    seq = pl.BlockSpec((1, tt, D_RNN), lambda i, j: (i, j, 0))
    per_b = lambda r: pl.BlockSpec((1, r, D_RNN), lambda i, j: (i, 0, 0))
    vec = _const_spec((1, D_RNN))
    wblk = _const_spec((RG_BLOCKS, RG_BLOCK, RG_BLOCK))
    return pl.pallas_call(
        functools.partial(_rglru_kernel, tt=tt),
        grid=(b, t // tt),
        in_specs=[seq, seq, per_b(CONV_W - 1), per_b(1), _const_spec((CONV_W, D_RNN)), vec,
                  wblk, vec, wblk, vec, vec],
        out_specs=[seq, per_b(CONV_W - 1), per_b(1)],
        out_shape=[jax.ShapeDtypeStruct((b, t, D_RNN), BF16),
                   jax.ShapeDtypeStruct((b, CONV_W - 1, D_RNN), F32),
                   jax.ShapeDtypeStruct((b, 1, D_RNN), F32)],
        scratch_shapes=[pltpu.VMEM((tt + 8, D_RNN), F32), pltpu.VMEM((1, D_RNN), F32)],
        compiler_params=_params(("parallel", "arbitrary")),
        name="rglru",
    )(rx, rgate, conv_state, h0, conv_w, conv_b, w_a, b_a, w_x, b_x, lam)


def _dsa_kernel(q_ref, qi_ref, wi_ref, kt_ref, v_ref, kie_ref, kio_ref, nb_ref, o_ref,
                key_s, wb_s, qs_s, m_s, l_s, acc_s, *, tq, past, l_real, n_sel):
    kb = DSA_BLK
    nc = kb // LANES
    i = pl.program_id(1)
    q0 = past + i * tq
    own = q0 // kb
    nkb = own + 1
    n_unit = (nkb + 1) // 2

    def lanes_at(k0):
        return lax.shift_right_logical(k0, LANE_SHIFT)

    wi = wi_ref[0] * (IDX_HEADS ** -0.5 * IDX_DIM ** -0.5)
    for h in range(IDX_HEADS):
        wb_s[h] = jnp.broadcast_to(wi[:, h:h + 1], (tq, LANES))
    q_chunk = (q0 + lax.broadcasted_iota(I32, (tq, LANES), 0)) >> CHUNK_SHIFT
    lane = lax.broadcasted_iota(I32, (tq, LANES), 1)

    def score_block(j, carry):
        k0 = pl.multiple_of(j * kb, kb)
        acc = [jnp.zeros((tq, LANES), F32) for _ in range(nc)]
        for h2 in range(IDX_HEADS // 2):
            qpair = qi_ref[0, :, h2 * LANES:(h2 + 1) * LANES]
            for par, kref in ((0, kie_ref), (1, kio_ref)):
                s = _dot(qpair, kref[0, :, pl.ds(k0, kb)])
                w = wb_s[2 * h2 + par]
                for c in range(nc):
                    acc[c] = acc[c] + jnp.maximum(s[:, c * LANES:(c + 1) * LANES], 0.0) * w
        for c in range(nc):
            bits = pltpu.bitcast(acc[c], I32)
            key = bits ^ ((bits >> 31) & 0x7FFFFFFF)
            kpos = k0 + c * LANES + lane
            key = jnp.where((kpos >> CHUNK_SHIFT) <= q_chunk, key, INT_MIN)
            key = jnp.where(kpos < l_real, key, INT_MIN)
            key_s[lanes_at(k0 + c * LANES)] = key
        return carry

    lax.fori_loop(0, nkb, score_block, 0)

    @pl.when(nkb % 2 == 1)
    def _():
        for c in range(nc):
            key_s[lanes_at(nkb * kb + c * LANES)] = jnp.full((tq, LANES), INT_MIN, I32)

    unit = 2 * kb
    strip = min(tq, SEARCH_ROWS)
    n_strip = tq // strip

    def search_strip(s):
        rows = slice(s * strip, (s + 1) * strip)

        def count_ge(cand):
            def body(u, cnt):
                for c in range(unit // LANES):
                    blk = key_s[lanes_at(u * unit + c * LANES), rows, :]
                    cnt = cnt + jnp.where(blk >= cand, 1.0, 0.0)
                return cnt

            cnt = lax.fori_loop(0, n_unit, body, jnp.zeros((strip, LANES), F32))
            return jnp.broadcast_to(jnp.sum(cnt, axis=1, keepdims=True), (strip, LANES))

        n_adm = jnp.minimum((q_chunk[rows] + 1) * CHUNK, l_real).astype(F32)

        def step(carry):
            it, res, cnt, _ = carry
            cand = jnp.where(it == 0, 0, res | (1 << (31 - it)))
            c = count_ge(cand)
            take = c >= n_sel
            res = jnp.where(take, cand, res)
            cnt = jnp.where(take, c, cnt)
            return it + 1, res, cnt, jnp.max(cnt) > n_sel

        init = (jnp.int32(0), jnp.full((strip, LANES), INT_MIN, I32), n_adm, jnp.max(n_adm) > n_sel)
        return lax.while_loop(lambda c: (c[0] < 32) & c[3], step, init)[1]

    res = [search_strip(s) for s in range(n_strip)]
    thr = jnp.maximum(jnp.concatenate(res, axis=0), INT_MIN + 1)

    for g in range(N_KV_HEADS):
        for hh in range(KV_GROUP):
            h = g * KV_GROUP + hh
            qs_s[g, hh * tq:(hh + 1) * tq, :] = q_ref[0, :, h * HEAD_DIM:(h + 1) * HEAD_DIM]
    m_s[...] = jnp.full(m_s.shape, NEG, F32)
    l_s[...] = jnp.zeros(l_s.shape, F32)
    acc_s[...] = jnp.zeros(acc_s.shape, F32)
    scale2 = HEAD_DIM ** -0.5 * LOG2E

    def attend(k0, width, near):
        ncw = width // LANES
        mb = [jnp.where(key_s[lanes_at(k0 + c * LANES)] >= thr, 0.0, NEG) for c in range(ncw)]
        for g in range(N_KV_HEADS):
            s_all = _dot(qs_s[g], kt_ref[0, g * HEAD_DIM:(g + 1) * HEAD_DIM, pl.ds(k0, width)]) * scale2
            vb = v_ref[0, pl.ds(k0, width), g * HEAD_DIM:(g + 1) * HEAD_DIM]
            for hh in range(KV_GROUP):
                h = g * KV_GROUP + hh
                s = s_all[hh * tq:(hh + 1) * tq, :]
                sc = []
                for c in range(ncw):
                    x = s[:, c * LANES:(c + 1) * LANES] + mb[c]
                    if near is not None:
                        x = x + nb_ref[near, h, :, c * LANES:(c + 1) * LANES]
                    sc.append(x)
                mx = sc[0]
                for c in range(1, ncw):
                    mx = jnp.maximum(mx, sc[c])
                m_old = m_s[h]
                m_new = jnp.maximum(m_old, jnp.broadcast_to(jnp.max(mx, axis=1, keepdims=True), (tq, LANES)))
                alpha = jnp.exp2(m_old - m_new)
                p = [jnp.exp2(x - m_new) for x in sc]
                ps = p[0]
                for c in range(1, ncw):
                    ps = ps + p[c]
                l_s[h] = alpha * l_s[h] + jnp.broadcast_to(jnp.sum(ps, axis=1, keepdims=True), (tq, LANES))
                m_s[h] = m_new
                pb = jnp.concatenate(p, axis=1).astype(BF16)
                acc_s[h] = alpha * acc_s[h] + _dot(pb, vb)

    n_far = jnp.maximum(own - 1, 0)
    per_wide = DSA_WIDE // kb
    n_wide = n_far // per_wide

    def wide_step(j, carry):
        attend(pl.multiple_of(j * DSA_WIDE, DSA_WIDE), DSA_WIDE, None)
        return carry

    def far_block(j, carry):
        attend(pl.multiple_of(j * kb, kb), kb, None)
        return carry

    lax.fori_loop(0, n_wide, wide_step, 0)
    lax.fori_loop(n_wide * per_wide, n_far, far_block, 0)

    @pl.when(own >= 1)
    def _():
        attend(pl.multiple_of((own - 1) * kb, kb), kb, 0)

    attend(pl.multiple_of(own * kb, kb), kb, 1)

    for h in range(N_HEADS):
        o_ref[0, :, h * HEAD_DIM:(h + 1) * HEAD_DIM] = (acc_s[h] / l_s[h]).astype(BF16)


def _t5_bucket(rel):
    nb = NUM_BUCKETS // 2
    max_exact = nb // 2
    ret = jnp.where(rel > 0, nb, 0)
    n = jnp.abs(rel)
    n_f = jnp.maximum(n, 1).astype(F32)
    large = max_exact + (jnp.log(n_f / max_exact) / math.log(MAX_DISTANCE / max_exact)
                         * (nb - max_exact)).astype(I32)
    large = jnp.minimum(large, nb - 1)
    return ret + jnp.where(n < max_exact, n, large)


def _near_bias(rel_bias, tq):
    a = jnp.arange(tq, dtype=I32)[:, None]
    b = jnp.arange(DSA_BLK, dtype=I32)[None, :]
    centered = (rel_bias - rel_bias[NUM_BUCKETS // 2 - 1]) * LOG2E
    tiles = []
    for d in (-DSA_BLK, 0):
        bucket = jnp.where(d + b - a > -MAX_DISTANCE, _t5_bucket(d + b - a), NUM_BUCKETS // 2 - 1)
        t = jnp.zeros((N_HEADS, tq, DSA_BLK), F32)
        for c in range(NUM_BUCKETS):
            t = jnp.where((bucket == c)[None], centered[c][:, None, None], t)
        tiles.append(t)
    return jnp.stack(tiles)


def _dsa(q, qi, wi, kt, v, kie, kio, nbias, *, past, l_real, n_sel):
    b, t, _ = q.shape
    tq = min(t, DSA_BLK)
    lpad = kt.shape[2]
    tile = lambda w: pl.BlockSpec((1, tq, w), lambda i, j: (i, j, 0))
    per_b = lambda r, c: pl.BlockSpec((1, r, c), lambda i, j: (i, 0, 0), pipeline_mode=pl.Buffered(1))
    kvd = N_KV_HEADS * HEAD_DIM
    return pl.pallas_call(
        functools.partial(_dsa_kernel, tq=tq, past=past, l_real=l_real, n_sel=n_sel),
        grid=(b, t // tq),
        in_specs=[tile(D_MODEL), tile(IDX_HEADS * IDX_DIM), tile(IDX_HEADS),
                  per_b(kvd, lpad), per_b(lpad, kvd), per_b(LANES, lpad), per_b(LANES, lpad),
                  _const_spec(nbias.shape)],
        out_specs=tile(D_MODEL),
        out_shape=jax.ShapeDtypeStruct((b, t, D_MODEL), BF16),
        scratch_shapes=[pltpu.VMEM((lpad // LANES, tq, LANES), I32),
                        pltpu.VMEM((IDX_HEADS, tq, LANES), F32),
                        pltpu.VMEM((N_KV_HEADS, KV_GROUP * tq, HEAD_DIM), BF16),
                        pltpu.VMEM((N_HEADS, tq, LANES), F32),
                        pltpu.VMEM((N_HEADS, tq, LANES), F32),
                        pltpu.VMEM((N_HEADS, tq, HEAD_DIM), F32)],
        compiler_params=_params(("parallel", "arbitrary")),
        name="dsa",
    )(q, qi, wi, kt, v, kie, kio, nbias)


def _merge_kernel(x_ref, hg_ref, at_ref, ga_ref, gb_ref, wa_ref, wb_ref, wo_ref, gf_ref, wq_ref, sk_ref,
                  x1_ref, xnt_ref, st_ref):
    ya = _dot(hg_ref[...], wa_ref[...])
    yb = _dot(at_ref[...], wb_ref[...])
    m = _sigmoid(ga_ref[...]) * ya + _sigmoid(gb_ref[...]) * yb
    x1 = x_ref[...] + _dot(m.astype(BF16), wo_ref[...])
    x1_ref[...] = x1
    xn = _rms(x1, gf_ref[...])
    xnt_ref[...] = xn.T.astype(BF16)
    qp = _dot(xn.astype(BF16), wq_ref[...])
    for j in range(2 * PEER_HEADS):
        qj = qp[:, j * LANES:(j + 1) * LANES].astype(BF16)
        st_ref[j] = _dot_nt(sk_ref[j], qj)


def _merge(x2, hg, attn, ga, gb, w_a_out, w_b_out, w_o, g_ffn, w_q, sk):
    n = x2.shape[0]
    tok = lambda w: pl.BlockSpec((TM, w), lambda i: (i, 0))
    sq = _const_spec((D_MODEL, D_MODEL))
    nsk = 2 * PEER_HEADS
    return pl.pallas_call(
        _merge_kernel,
        grid=(n // TM,),
        in_specs=[tok(D_MODEL)] * 5 + [sq, sq, sq, _const_spec((1, D_MODEL)),
                                       _const_spec(w_q.shape), _const_spec(sk.shape)],
        out_specs=[tok(D_MODEL), pl.BlockSpec((D_MODEL, TM), lambda i: (0, i)),
                   pl.BlockSpec((nsk, PEER_NKEYS, TM), lambda i: (0, 0, i))],
        out_shape=[jax.ShapeDtypeStruct((n, D_MODEL), F32), jax.ShapeDtypeStruct((D_MODEL, n), BF16),
                   jax.ShapeDtypeStruct((nsk, PEER_NKEYS, n), F32)],
        compiler_params=_params(("parallel",)),
        name="merge",
    )(x2, hg, attn, ga, gb, w_a_out, w_b_out, w_o, g_ffn, w_q, sk)


_PAIR_LIMIT = [PEER_TOPK // (i + 1) for i in range(PEER_TOPK)]
_NOT_TOP = 99.0


def _peer_select_kernel(st_ref, r2_ref, e2_ref, n1_ref, e1_ref):
    ninf = -jnp.inf
    sub = 8
    row8 = lax.broadcasted_iota(I32, (sub, PEER_TM), 0)

    def top(s):
        vals, cur = [], s
        rank = jnp.full(s.shape, _NOT_TOP, F32)
        for k in range(PEER_TOPK):
            m = jnp.max(cur, axis=0, keepdims=True)
            vals.append(m)
            sel = cur == m
            rank = jnp.where(sel, float(k), rank)
            cur = jnp.where(sel, ninf, cur)
        return vals, rank

    def head(h, carry):
        s1 = st_ref[2 * h]
        s2 = st_ref[2 * h + 1]
        v1, r1 = top(s1)
        v2, r2 = top(s2)
        v2t = jnp.concatenate(v2, axis=0)
        v1_low = jnp.concatenate(v1[sub:], axis=0)
        groups = [v1[0] + v2t]
        for i in range(1, sub):
            groups.append(jnp.where(row8 < _PAIR_LIMIT[i], v1[i] + v2t[:sub], ninf))
        groups.append(v1_low + v2[0])
        cand = jnp.concatenate(groups, axis=0)
        cur = cand
        for r in range(PEER_TOPK):
            thr = jnp.max(cur, axis=0, keepdims=True)
            if r + 1 < PEER_TOPK:
                cur = jnp.where(cur == thr, ninf, cur)
        picked = cand >= thr
        z = jnp.sum(jnp.where(picked, jnp.exp(cand - (v1[0] + v2[0])), 0.0), axis=0, keepdims=True)
        cnt = jnp.where(picked, 1.0, 0.0)
        n_of_rank = [jnp.sum(cnt[:PEER_TOPK], axis=0, keepdims=True)]
        for i in range(1, sub):
            lo = PEER_TOPK + (i - 1) * sub
            n_of_rank.append(jnp.sum(cnt[lo:lo + sub], axis=0, keepdims=True))
        lo = PEER_TOPK + (sub - 1) * sub
        for i in range(sub, PEER_TOPK):
            n_of_rank.append(cnt[lo + i - sub:lo + i - sub + 1])
        n1 = jnp.zeros(s1.shape, F32)
        for i in range(PEER_TOPK):
            n1 = jnp.where(r1 == float(i), n_of_rank[i], n1)
        r2_ref[h] = r2.astype(BF16)
        e2_ref[h] = (jnp.exp(s2 - v2[0]) / z).astype(BF16)
        n1_ref[h] = n1
        e1_ref[h] = jnp.exp(s1 - v1[0])
        return carry

    lax.fori_loop(0, PEER_HEADS, head, 0)


def _peer_select(st):
    n = st.shape[2]
    big = pl.BlockSpec((PEER_HEADS, PEER_NKEYS, PEER_TM), lambda i: (0, 0, i))
    sds = lambda dt: jax.ShapeDtypeStruct((PEER_HEADS, PEER_NKEYS, n), dt)
    return pl.pallas_call(
        _peer_select_kernel,
        grid=(n // PEER_TM,),
        in_specs=[pl.BlockSpec((2 * PEER_HEADS, PEER_NKEYS, PEER_TM), lambda i: (0, 0, i))],
        out_specs=[big, big, big, big],
        out_shape=[sds(BF16), sds(BF16), sds(F32), sds(F32)],
        compiler_params=_params(("parallel",)),
        name="peer_select",
    )(st)


def _gelu_folded(x):
    c = math.sqrt(2.0 / math.pi)
    half = 0.5 * x
    return half + half * jnp.tanh(x * (c + (c * 0.044715) * (x * x)))


def _peer_dense_kernel(xnt_ref, u_ref, vt_ref, r2_ref, e2_ref, n1_ref, e1_ref, o_ref, acc_s):
    e = pl.program_id(1)
    pack = 16
    n_pack = PEER_NKEYS // pack
    per_chunk = PEER_EB // PEER_NKEYS

    @pl.when(e == 0)
    def _():
        acc_s[...] = jnp.zeros(acc_s.shape, F32)

    def gate_weights(a):
        w = [jnp.zeros((pack, PEER_TM), BF16) for _ in range(n_pack)]
        for h in range(PEER_HEADS):
            n1 = jnp.broadcast_to(n1_ref[h, pl.ds(a, 1), :], (pack, PEER_TM)).astype(BF16)
            e1 = jnp.broadcast_to(e1_ref[h, pl.ds(a, 1), :], (pack, PEER_TM)).astype(BF16)
            for r in range(n_pack):
                rows = slice(r * pack, (r + 1) * pack)
                w[r] = w[r] + jnp.where(r2_ref[h, rows, :] < n1, e2_ref[h, rows, :], 0.0) * e1
        return jnp.concatenate(w, axis=0)

    xnt = xnt_ref[...]
    total = None
    for k in range(PEER_CHUNKS):
        a0 = (e * PEER_CHUNKS + k) * per_chunk
        w = [gate_weights(a0 + aa) for aa in range(per_chunk)]
        act_t = _dot(u_ref[k * PEER_EB:(k + 1) * PEER_EB, :], xnt)
        g = [w[aa] * _gelu_folded(act_t[aa * PEER_NKEYS:(aa + 1) * PEER_NKEYS, :]).astype(BF16)
             for aa in range(per_chunk)]
        part = _dot(vt_ref[:, k * PEER_EB:(k + 1) * PEER_EB], jnp.concatenate(g, axis=0))
        total = part if total is None else total + part
    acc_s[...] += total

    @pl.when(e == pl.num_programs(1) - 1)
    def _():
        o_ref[...] = acc_s[...].T


def _peer_dense(xnt, u_b, vt_b, r2, e2, n1, e1):
    n = xnt.shape[1]
    step = PEER_EB * PEER_CHUNKS
    big = pl.BlockSpec((PEER_HEADS, PEER_NKEYS, PEER_TM), lambda i, e: (0, 0, i))
    return pl.pallas_call(
        _peer_dense_kernel,
        grid=(n // PEER_TM, PEER_EXPERTS // step),
        in_specs=[pl.BlockSpec((D_MODEL, PEER_TM), lambda i, e: (0, i)),
                  pl.BlockSpec((step, D_MODEL), lambda i, e: (e, 0)),
                  pl.BlockSpec((D_MODEL, step), lambda i, e: (0, e)),
                  big, big, big, big],
        out_specs=pl.BlockSpec((PEER_TM, D_MODEL), lambda i, e: (i, 0)),
        out_shape=jax.ShapeDtypeStruct((n, D_MODEL), F32),
        scratch_shapes=[pltpu.VMEM((D_MODEL, PEER_TM), F32)],
        compiler_params=_params(("parallel", "arbitrary")),
        name="peer_dense",
    )(xnt, u_b, vt_b, r2, e2, n1, e1)


def _final_kernel(x1_ref, po_ref, p_ref, gp_ref, wg_ref, wp_ref, gfin_ref, y_ref):
    x2 = x1_ref[...] + po_ref[...]
    gate = _sigmoid(_dot(_rms(x2, gp_ref[...]).astype(BF16), wg_ref[...]))
    x3 = x2 + gate * _dot(p_ref[...].astype(BF16), wp_ref[...])
    y_ref[...] = _rms(x3, gfin_ref[...])


def _final(x1, po, p2, g_ple, w_gate, w_proj, g_final):
    n = x1.shape[0]
    tok = lambda w: pl.BlockSpec((TM, w), lambda i: (i, 0))
    vec = _const_spec((1, D_MODEL))
    return pl.pallas_call(
        _final_kernel,
        grid=(n // TM,),
        in_specs=[tok(D_MODEL), tok(D_MODEL), tok(D_PLE), vec, _const_spec((D_MODEL, D_MODEL)),
                  _const_spec((D_PLE, D_MODEL)), vec],
        out_specs=tok(D_MODEL),
        out_shape=jax.ShapeDtypeStruct((n, D_MODEL), F32),
        compiler_params=_params(("parallel",)),
        name="final",
    )(x1, po, p2, g_ple, w_gate, w_proj, g_final)


def _pad_keys(x, lpad, axis):
    pad = lpad - x.shape[axis]
    if pad == 0:
        return x
    widths = [(0, 0)] * x.ndim
    widths[axis] = (0, pad)
    return jnp.pad(x, widths)


def _layer(x, p, conv_state, h_state, cache_k, cache_v, cache_ki, w):
    b, t, _ = x.shape
    n = b * t
    kvd = N_KV_HEADS * HEAD_DIM
    (rx, rgate, q, k, v, kb, vb, qi, ga, gb, ki, kib, wi, *key_major) = _inproj(
        x.reshape(n, D_MODEL), w["g_mix"], w["w_in"], t)

    hg, conv_new, h_last = _rglru(rx.reshape(b, t, D_RNN), rgate.reshape(b, t, D_RNN), conv_state,
                                  h_state.reshape(b, 1, D_RNN), w["conv_w"], w["conv_b"], w["w_rg_a"],
                                  w["b_rg_a"], w["w_rg_x"], w["b_rg_x"], w["rg_lambda"])

    k_all, v_all, ki_all = kb.reshape(b, t, kvd), vb.reshape(b, t, kvd), kib.reshape(b, t, IDX_DIM)
    past = 0
    if cache_k is not None:
        past = cache_k.shape[1]
        k_all = jnp.concatenate([cache_k.reshape(b, past, kvd).astype(BF16), k_all], axis=1)
        v_all = jnp.concatenate([cache_v.reshape(b, past, kvd).astype(BF16), v_all], axis=1)
        ki_all = jnp.concatenate([cache_ki.astype(BF16), ki_all], axis=1)
    l_real = past + t
    assert past % DSA_BLK == 0 and t % min(t, DSA_BLK) == 0 and n % PEER_TM == 0
    lpad = -(-l_real // (2 * DSA_BLK)) * (2 * DSA_BLK)
    if past == 0 and key_major and lpad == t:
        kt, kie, kio = key_major
    else:
        kt = _pad_keys(jnp.swapaxes(k_all, 1, 2), lpad, 2)
        kit = _pad_keys(jnp.swapaxes(ki_all, 1, 2), lpad, 2)
        zeros = jnp.zeros_like(kit)
        kie = jnp.concatenate([kit, zeros], axis=1)
        kio = jnp.concatenate([zeros, kit], axis=1)
    v_all = _pad_keys(v_all, lpad, 1)
    tq = min(t, DSA_BLK)
    attn = _dsa(q.reshape(b, t, D_MODEL), qi.reshape(b, t, IDX_HEADS * IDX_DIM), wi.reshape(b, t, IDX_HEADS),
                kt, v_all, kie, kio, _near_bias(w["rel_bias"], tq),
                past=past, l_real=l_real, n_sel=min(TOPK_MAX, l_real // 4))
    x1, xnt, st = _merge(x.reshape(n, D_MODEL), hg.reshape(n, D_RNN), attn.reshape(n, D_MODEL), ga, gb,
                        w["w_a_out"], w["w_b_out"], w["w_o"], w["g_ffn"], w["w_peer_q"], w["peer_sk"])
    r2, e2, n1, e1 = _peer_select(st)
    po = _peer_dense(xnt, w["peer_u"], w["peer_vt"], r2, e2, n1, e1)
    y = _final(x1, po, p.reshape(n, D_PLE), w["g_ple"], w["w_ple_gate"], w["w_ple_proj"], w["g_final"])
    return (y.reshape(b, t, D_MODEL), k.reshape(b, t, N_KV_HEADS, HEAD_DIM),
            v.reshape(b, t, N_KV_HEADS, HEAD_DIM), ki.reshape(b, t, IDX_DIM), conv_new,
            h_last.reshape(b, D_RNN))


def kernel(x_prompt, x_sample, p_prompt, p_sample, state_conv, state_rglru, cache_k, cache_v, cache_idx_k, rel_bias, g_mix, w_in, conv_w, conv_b, w_rg_a, b_rg_a, w_rg_x, b_rg_x, rg_lambda, w_a_out, w_b_out, w_o, g_ffn, w_peer_q, peer_sub_keys, peer_u, peer_v, g_ple, w_ple_gate, w_ple_proj, g_final):
    assert g_mix.shape[0] == 1, "single trunk layer"
    row = lambda a: a.reshape(1, -1)
    wi_full = w_in[0]
    cut = _C_GA
    tail = IDX_DIM + IDX_HEADS
    w_r = jnp.concatenate([wi_full[:, :cut], wi_full[:, cut + tail:], wi_full[:, cut:cut + tail],
                           jnp.zeros((D_MODEL, _C_END - _C_TAIL - tail), F32)], axis=1).astype(BF16)
    w = dict(
        g_mix=row(g_mix[0]), w_in=w_r, conv_w=conv_w[0], conv_b=row(conv_b[0]),
        w_rg_a=w_rg_a[0].astype(BF16), b_rg_a=row(b_rg_a[0]), w_rg_x=w_rg_x[0].astype(BF16),
        b_rg_x=row(b_rg_x[0]), rg_lambda=row(rg_lambda[0]), rel_bias=rel_bias,
        w_a_out=w_a_out[0].astype(BF16), w_b_out=w_b_out[0].astype(BF16), w_o=w_o[0].astype(BF16),
        g_ffn=row(g_ffn[0]), w_peer_q=w_peer_q[0].astype(BF16),
        peer_sk=peer_sub_keys[0].reshape(2 * PEER_HEADS, PEER_NKEYS, PEER_DK // 2).astype(BF16),
        peer_u=peer_u[0].astype(BF16), peer_vt=peer_v[0].astype(BF16).T,
        g_ple=row(g_ple[0]), w_ple_gate=w_ple_gate[0].astype(BF16), w_ple_proj=w_ple_proj[0].astype(BF16),
        g_final=row(g_final),
    )
    bp = x_prompt.shape[0]
    zc = jnp.zeros((bp, CONV_W - 1, D_RNN), F32)
    zh = jnp.zeros((bp, D_RNN), F32)
    yp, k1, v1, ki1, c1, r1 = _layer(x_prompt, p_prompt[0], zc, zh, None, None, None, w)
    ys, k2, v2, ki2, c2, r2 = _layer(x_sample, p_sample[0], state_conv[0], state_rglru[0],
                                     cache_k[0], cache_v[0], cache_idx_k[0], w)
    return (yp, ys, k1[None], v1[None], ki1[None], c1[None], r1[None],
            k2[None], v2[None], ki2[None], c2[None], r2[None])
```

```python
import functools
import math

import jax
import jax.numpy as jnp
from jax import lax
from jax.experimental import pallas as pl
from jax.experimental.pallas import tpu as pltpu

F32 = jnp.float32
BF16 = jnp.bfloat16
I32 = jnp.int32
LOG2E = math.log2(math.e)

D_MODEL = 1024
CHUNK = 64
CHUNK_SHIFT = 6
D_PLE = 256
D_RNN = 1024
RG_BLOCKS = 8
RG_BLOCK = D_RNN // RG_BLOCKS
CONV_W = 4
RG_C = 8.0
N_HEADS = 8
HEAD_DIM = 128
N_KV_HEADS = 2
KV_GROUP = N_HEADS // N_KV_HEADS
IDX_HEADS = 16
IDX_DIM = 64
TOPK_MAX = 256
NUM_BUCKETS = 32
MAX_DISTANCE = 128
PEER_HEADS = 8
PEER_NKEYS = 128
PEER_EXPERTS = PEER_NKEYS * PEER_NKEYS
PEER_DK = 256
PEER_TOPK = 16
EPS = 1e-6
NEG = -1e30
INT_MIN = -(2 ** 31)
Q_SCALE = HEAD_DIM ** -0.5 * LOG2E

LANES = 128
LANE_SHIFT = 7
VMEM_LIMIT = 56 * 1024 * 1024

_C_RX, _C_RG, _C_Q, _C_K, _C_V, _C_QI, _C_GA, _C_GB, _C_TAIL, _C_END = (
    0, 1024, 2048, 3072, 3328, 3584, 4608, 5632, 6656, 6784)

TM = 256
DSA_BLK = 256
DSA_WIDE = 1024
SEARCH_ROWS = 128
PEER_TM = 512
PEER_EB = 512
PEER_CHUNKS = 4


def _params(sem):
    return pltpu.CompilerParams(dimension_semantics=sem, vmem_limit_bytes=VMEM_LIMIT)


def _const_spec(shape):
    nd = len(shape)
    return pl.BlockSpec(shape, lambda *_: (0,) * nd, pipeline_mode=pl.Buffered(1))


def _rms(x, g):
    return x * lax.rsqrt(jnp.mean(x * x, axis=-1, keepdims=True) + EPS) * g


def _gelu(x):
    return 0.5 * x * (1.0 + jnp.tanh(math.sqrt(2.0 / math.pi) * (x + 0.044715 * (x * x * x))))


def _sigmoid(x):
    return 1.0 / (1.0 + jnp.exp(-x))


def _dot(a, b):
    return jnp.dot(a, b, preferred_element_type=F32)


def _dot_nt(a, b):
    return lax.dot_general(a, b, (((1,), (1,)), ((), ())), preferred_element_type=F32)


def _inproj_kernel(x_ref, g_ref, w_ref, rx_ref, rg_ref, q_ref, k_ref, v_ref, kb_ref, vb_ref,
                   qi_ref, ga_ref, gb_ref, ki_ref, kib_ref, wi_ref, *key_major_refs):
    n = _rms(x_ref[...], g_ref[...]).astype(BF16)

    def mm(a, b):
        return _dot(n, w_ref[:, a:b])

    rx_ref[...] = mm(_C_RX, _C_RG)
    rg_ref[...] = mm(_C_RG, _C_Q)
    q_ref[...] = (mm(_C_Q, _C_K) * Q_SCALE).astype(BF16)
    k = mm(_C_K, _C_V)
    k_ref[...] = k
    kb_ref[...] = k.astype(BF16)
    v = mm(_C_V, _C_QI)
    v_ref[...] = v
    vb_ref[...] = v.astype(BF16)
    qi_ref[...] = mm(_C_QI, _C_GA).astype(BF16)
    ga_ref[...] = mm(_C_GA, _C_GB)
    gb_ref[...] = mm(_C_GB, _C_TAIL)
    tail = mm(_C_TAIL, _C_END)
    ki = tail[:, :IDX_DIM]
    ki_ref[...] = ki
    kib_ref[...] = ki.astype(BF16)
    wi_ref[...] = tail[:, IDX_DIM:IDX_DIM + IDX_HEADS]
    if key_major_refs:
        kt_ref, kie_ref, kio_ref = key_major_refs
        kt_ref[0] = k.T.astype(BF16)
        tail_t = tail.T
        kit = tail_t[:IDX_DIM].astype(BF16)
        zeros = jnp.zeros_like(kit)
        kie_ref[0] = jnp.concatenate([kit, zeros], axis=0)
        kio_ref[0] = jnp.concatenate([zeros, kit], axis=0)


def _inproj(x2, g_mix, w_r, seq_len):
    n = x2.shape[0]
    kvd = N_KV_HEADS * HEAD_DIM
    tok = lambda w: pl.BlockSpec((TM, w), lambda i: (i, 0))
    extra_specs, extra_shapes = [], []
    if seq_len % TM == 0:
        per_seq = seq_len // TM
        kmaj = lambda r: pl.BlockSpec((1, r, TM), lambda i: (i // per_seq, 0, i % per_seq))
        extra_specs = [kmaj(kvd), kmaj(LANES), kmaj(LANES)]
        extra_shapes = [jax.ShapeDtypeStruct((n // seq_len, r, seq_len), BF16) for r in (kvd, LANES, LANES)]
    widths_dtypes = [(D_RNN, F32), (D_RNN, F32), (D_MODEL, BF16), (kvd, F32), (kvd, F32), (kvd, BF16),
                     (kvd, BF16), (IDX_HEADS * IDX_DIM, BF16), (D_MODEL, F32), (D_MODEL, F32),
                     (IDX_DIM, F32), (IDX_DIM, BF16), (IDX_HEADS, F32)]
    return pl.pallas_call(
        _inproj_kernel,
        grid=(n // TM,),
        in_specs=[tok(D_MODEL), _const_spec((1, D_MODEL)), _const_spec(w_r.shape)],
        out_specs=[tok(w) for w, _ in widths_dtypes] + extra_specs,
        out_shape=[jax.ShapeDtypeStruct((n, w), dt) for w, dt in widths_dtypes] + extra_shapes,
        compiler_params=_params(("parallel",)),
        name="inproj",
    )(x2, g_mix, w_r)


def _rglru_kernel(x_ref, gate_ref, cs_ref, h0_ref, cw_ref, cb_ref, wa_ref, ba_ref, wx_ref, bx_ref,
                  lam_ref, hg_ref, cnew_ref, hlast_ref, xp_s, hc_s, *, tt):
    @pl.when(pl.program_id(1) == 0)
    def _():
        xp_s[5:8, :] = cs_ref[0]
        hc_s[...] = h0_ref[0]

    xp_s[8:8 + tt, :] = x_ref[0]
    nl = -lam_ref[...]
    softplus = jnp.maximum(nl, 0.0) + jnp.log1p(jnp.exp(-jnp.abs(nl)))
    row = lax.broadcasted_iota(I32, (tt, RG_BLOCK), 0)
    for n in range(RG_BLOCKS):
        sl = slice(n * RG_BLOCK, (n + 1) * RG_BLOCK)
        xc = cb_ref[:, sl]
        for j in range(CONV_W):
            xc = xc + cw_ref[j:j + 1, sl] * xp_s[5 + j:5 + j + tt, sl]
        xcb = xc.astype(BF16)
        r = _sigmoid(_dot(xcb, wa_ref[n]) + ba_ref[:, sl])
        ig = _sigmoid(_dot(xcb, wx_ref[n]) + bx_ref[:, sl])
        log_a = -RG_C * r * softplus[:, sl]
        a = jnp.exp(log_a)
        mult = jnp.sqrt(-jnp.tanh(log_a) * (a * a + 1.0))
        bv = mult * (ig * xc)
        d = 1
        while d < tt:
            keep = row >= d
            a_sh = pltpu.roll(a, d, 0)
            b_sh = pltpu.roll(bv, d, 0)
            bv = jnp.where(keep, a * b_sh + bv, bv)
            a = jnp.where(keep, a * a_sh, a)
            d *= 2
        h = a * hc_s[:, sl] + bv
        hc_s[:, sl] = h[tt - 1:tt, :]
        hg_ref[0, :, sl] = (h * _gelu(gate_ref[0, :, sl])).astype(BF16)
    tail = xp_s[tt + 5:tt + 8, :]
    cnew_ref[0] = tail
    xp_s[5:8, :] = tail
    hlast_ref[0] = hc_s[...]


def _rglru(rx, rgate, conv_state, h0, conv_w, conv_b, w_a, b_a, w_x, b_x, lam):
    b, t, _ = rx.shape
    tt = min(t, 256)
    seq = pl.BlockSpec((1, tt, D_RNN), lambda i, j: (i, j, 0))
    per_b = lambda r: pl.BlockSpec((1, r, D_RNN), lambda i, j: (i, 0, 0))
    vec = _const_spec((1, D_RNN))
    wblk = _const_spec((RG_BLOCKS, RG_BLOCK, RG_BLOCK))
    return pl.pallas_call(
        functools.partial(_rglru_kernel, tt=tt),
        grid=(b, t // tt),
        in_specs=[seq, seq, per_b(CONV_W - 1), per_b(1), _const_spec((CONV_W, D_RNN)), vec,
                  wblk, vec, wblk, vec, vec],
        out_specs=[seq, per_b(CONV_W - 1), per_b(1)],
        out_shape=[jax.ShapeDtypeStruct((b, t, D_RNN), BF16),
                   jax.ShapeDtypeStruct((b, CONV_W - 1, D_RNN), F32),
                   jax.ShapeDtypeStruct((b, 1, D_RNN), F32)],
        scratch_shapes=[pltpu.VMEM((tt + 8, D_RNN), F32), pltpu.VMEM((1, D_RNN), F32)],
        compiler_params=_params(("parallel", "arbitrary")),
        name="rglru",
    )(rx, rgate, conv_state, h0, conv_w, conv_b, w_a, b_a, w_x, b_x, lam)


def _dsa_kernel(q_ref, qi_ref, wi_ref, kt_ref, v_ref, kie_ref, kio_ref, nb_ref, o_ref,
                key_s, wb_s, qs_s, m_s, l_s, acc_s, *, tq, past, l_real, n_sel):
    kb = DSA_BLK
    nc = kb // LANES
    i = pl.program_id(1)
    q0 = past + i * tq
    own = q0 // kb
    nkb = own + 1
    n_unit = (nkb + 1) // 2

    def lanes_at(k0):
        return lax.shift_right_logical(k0, LANE_SHIFT)

    wi = wi_ref[0] * (IDX_HEADS ** -0.5 * IDX_DIM ** -0.5)
    for h in range(IDX_HEADS):
        wb_s[h] = jnp.broadcast_to(wi[:, h:h + 1], (tq, LANES))
    q_chunk = (q0 + lax.broadcasted_iota(I32, (tq, LANES), 0)) >> CHUNK_SHIFT
    lane = lax.broadcasted_iota(I32, (tq, LANES), 1)

    def score_block(j, carry):
        k0 = pl.multiple_of(j * kb, kb)
        acc = [jnp.zeros((tq, LANES), F32) for _ in range(nc)]
        for h2 in range(IDX_HEADS // 2):
            qpair = qi_ref[0, :, h2 * LANES:(h2 + 1) * LANES]
            for par, kref in ((0, kie_ref), (1, kio_ref)):
                s = _dot(qpair, kref[0, :, pl.ds(k0, kb)])
                w = wb_s[2 * h2 + par]
                for c in range(nc):
                    acc[c] = acc[c] + jnp.maximum(s[:, c * LANES:(c + 1) * LANES], 0.0) * w
        for c in range(nc):
            bits = pltpu.bitcast(acc[c], I32)
            key = bits ^ ((bits >> 31) & 0x7FFFFFFF)
            kpos = k0 + c * LANES + lane
            key = jnp.where((kpos >> CHUNK_SHIFT) <= q_chunk, key, INT_MIN)
            key = jnp.where(kpos < l_real, key, INT_MIN)
            key_s[lanes_at(k0 + c * LANES)] = key
        return carry

    lax.fori_loop(0, nkb, score_block, 0)

    @pl.when(nkb % 2 == 1)
    def _():
        for c in range(nc):
            key_s[lanes_at(nkb * kb + c * LANES)] = jnp.full((tq, LANES), INT_MIN, I32)

    unit = 2 * kb
    strip = min(tq, SEARCH_ROWS)
    n_strip = tq // strip

    def search_strip(s):
        rows = slice(s * strip, (s + 1) * strip)

        def count_ge(cand):
            def body(u, cnt):
                for c in range(unit // LANES):
                    blk = key_s[lanes_at(u * unit + c * LANES), rows, :]
                    cnt = cnt + jnp.where(blk >= cand, 1.0, 0.0)
                return cnt

            cnt = lax.fori_loop(0, n_unit, body, jnp.zeros((strip, LANES), F32))
            return jnp.broadcast_to(jnp.sum(cnt, axis=1, keepdims=True), (strip, LANES))

        n_adm = jnp.minimum((q_chunk[rows] + 1) * CHUNK, l_real).astype(F32)

        def step(carry):
            it, res, cnt, _ = carry
            cand = jnp.where(it == 0, 0, res | (1 << (31 - it)))
            c = count_ge(cand)
            take = c >= n_sel
            res = jnp.where(take, cand, res)
            cnt = jnp.where(take, c, cnt)
            return it + 1, res, cnt, jnp.max(cnt) > n_sel

        init = (jnp.int32(0), jnp.full((strip, LANES), INT_MIN, I32), n_adm, jnp.max(n_adm) > n_sel)
        return lax.while_loop(lambda c: (c[0] < 32) & c[3], step, init)[1]

    res = [search_strip(s) for s in range(n_strip)]
    thr = jnp.maximum(jnp.concatenate(res, axis=0), INT_MIN + 1)

    for g in range(N_KV_HEADS):
        for hh in range(KV_GROUP):
            h = g * KV_GROUP + hh
            qs_s[g, hh * tq:(hh + 1) * tq, :] = q_ref[0, :, h * HEAD_DIM:(h + 1) * HEAD_DIM]
    m_s[...] = jnp.full(m_s.shape, NEG, F32)
    l_s[...] = jnp.zeros(l_s.shape, F32)
    acc_s[...] = jnp.zeros(acc_s.shape, F32)

    def attend(k0, width, near):
        ncw = width // LANES
        mb = [jnp.where(key_s[lanes_at(k0 + c * LANES)] >= thr, 0.0, NEG) for c in range(ncw)]
        for g in range(N_KV_HEADS):
            s_all = _dot(qs_s[g], kt_ref[0, g * HEAD_DIM:(g + 1) * HEAD_DIM, pl.ds(k0, width)])
            vb = v_ref[0, pl.ds(k0, width), g * HEAD_DIM:(g + 1) * HEAD_DIM]
            for hh in range(KV_GROUP):
                h = g * KV_GROUP + hh
                s = s_all[hh * tq:(hh + 1) * tq, :]
                sc = []
                for c in range(ncw):
                    x = s[:, c * LANES:(c + 1) * LANES] + mb[c]
                    if near is not None:
                        x = x + nb_ref[near, h, :, c * LANES:(c + 1) * LANES]
                    sc.append(x)
                mx = sc[0]
                for c in range(1, ncw):
                    mx = jnp.maximum(mx, sc[c])
                m_old = m_s[h]
                m_new = jnp.maximum(m_old, jnp.broadcast_to(jnp.max(mx, axis=1, keepdims=True), (tq, LANES)))
                alpha = jnp.exp2(m_old - m_new)
                p = [jnp.exp2(x - m_new) for x in sc]
                ps = p[0]
                for c in range(1, ncw):
                    ps = ps + p[c]
                l_s[h] = alpha * l_s[h] + jnp.broadcast_to(jnp.sum(ps, axis=1, keepdims=True), (tq, LANES))
                m_s[h] = m_new
                pb = jnp.concatenate(p, axis=1).astype(BF16)
                acc_s[h] = alpha * acc_s[h] + _dot(pb, vb)

    n_far = jnp.maximum(own - 1, 0)
    per_wide = DSA_WIDE // kb
    n_wide = n_far // per_wide

    def wide_step(j, carry):
        attend(pl.multiple_of(j * DSA_WIDE, DSA_WIDE), DSA_WIDE, None)
        return carry

    def far_block(j, carry):
        attend(pl.multiple_of(j * kb, kb), kb, None)
        return carry

    lax.fori_loop(0, n_wide, wide_step, 0)
    lax.fori_loop(n_wide * per_wide, n_far, far_block, 0)

    @pl.when(own >= 1)
    def _():
        attend(pl.multiple_of((own - 1) * kb, kb), kb, 0)

    attend(pl.multiple_of(own * kb, kb), kb, 1)

    for h in range(N_HEADS):
        o_ref[0, :, h * HEAD_DIM:(h + 1) * HEAD_DIM] = (acc_s[h] / l_s[h]).astype(BF16)


def _t5_bucket(rel):
    nb = NUM_BUCKETS // 2
    max_exact = nb // 2
    ret = jnp.where(rel > 0, nb, 0)
    n = jnp.abs(rel)
    n_f = jnp.maximum(n, 1).astype(F32)
    large = max_exact + (jnp.log(n_f / max_exact) / math.log(MAX_DISTANCE / max_exact)
                         * (nb - max_exact)).astype(I32)
    large = jnp.minimum(large, nb - 1)
    return ret + jnp.where(n < max_exact, n, large)


def _near_bias(rel_bias, tq):
    a = jnp.arange(tq, dtype=I32)[:, None]
    b = jnp.arange(DSA_BLK, dtype=I32)[None, :]
    centered = (rel_bias - rel_bias[NUM_BUCKETS // 2 - 1]) * LOG2E
    tiles = []
    for d in (-DSA_BLK, 0):
        bucket = jnp.where(d + b - a > -MAX_DISTANCE, _t5_bucket(d + b - a), NUM_BUCKETS // 2 - 1)
        t = jnp.zeros((N_HEADS, tq, DSA_BLK), F32)
        for c in range(NUM_BUCKETS):
            t = jnp.where((bucket == c)[None], centered[c][:, None, None], t)
        tiles.append(t)
    return jnp.stack(tiles)


def _dsa(q, qi, wi, kt, v, kie, kio, nbias, *, past, l_real, n_sel):
    b, t, _ = q.shape
    tq = min(t, DSA_BLK)
    lpad = kt.shape[2]
    tile = lambda w: pl.BlockSpec((1, tq, w), lambda i, j: (i, j, 0))
    per_b = lambda r, c: pl.BlockSpec((1, r, c), lambda i, j: (i, 0, 0), pipeline_mode=pl.Buffered(1))
    kvd = N_KV_HEADS * HEAD_DIM
    return pl.pallas_call(
        functools.partial(_dsa_kernel, tq=tq, past=past, l_real=l_real, n_sel=n_sel),
        grid=(b, t // tq),
        in_specs=[tile(D_MODEL), tile(IDX_HEADS * IDX_DIM), tile(IDX_HEADS),
                  per_b(kvd, lpad), per_b(lpad, kvd), per_b(LANES, lpad), per_b(LANES, lpad),
                  _const_spec(nbias.shape)],
        out_specs=tile(D_MODEL),
        out_shape=jax.ShapeDtypeStruct((b, t, D_MODEL), BF16),
        scratch_shapes=[pltpu.VMEM((lpad // LANES, tq, LANES), I32),
                        pltpu.VMEM((IDX_HEADS, tq, LANES), F32),
                        pltpu.VMEM((N_KV_HEADS, KV_GROUP * tq, HEAD_DIM), BF16),
                        pltpu.VMEM((N_HEADS, tq, LANES), F32),
                        pltpu.VMEM((N_HEADS, tq, LANES), F32),
                        pltpu.VMEM((N_HEADS, tq, HEAD_DIM), F32)],
        compiler_params=_params(("parallel", "arbitrary")),
        name="dsa",
    )(q, qi, wi, kt, v, kie, kio, nbias)


def _merge_kernel(x_ref, hg_ref, at_ref, ga_ref, gb_ref, wa_ref, wb_ref, wo_ref, gf_ref, wq_ref, sk_ref,
                  x1_ref, xnt_ref, st_ref):
    ya = _dot(hg_ref[...], wa_ref[...])
    yb = _dot(at_ref[...], wb_ref[...])
    m = _sigmoid(ga_ref[...]) * ya + _sigmoid(gb_ref[...]) * yb
    x1 = x_ref[...] + _dot(m.astype(BF16), wo_ref[...])
    x1_ref[...] = x1
    xn = _rms(x1, gf_ref[...])
    xnt_ref[...] = xn.T.astype(BF16)
    qp = _dot(xn.astype(BF16), wq_ref[...])
    for j in range(2 * PEER_HEADS):
        qj = qp[:, j * LANES:(j + 1) * LANES].astype(BF16)
        st_ref[j] = _dot_nt(sk_ref[j], qj)


def _merge(x2, hg, attn, ga, gb, w_a_out, w_b_out, w_o, g_ffn, w_q, sk):
    n = x2.shape[0]
    tok = lambda w: pl.BlockSpec((TM, w), lambda i: (i, 0))
    sq = _const_spec((D_MODEL, D_MODEL))
    nsk = 2 * PEER_HEADS
    return pl.pallas_call(
        _merge_kernel,
        grid=(n // TM,),
        in_specs=[tok(D_MODEL)] * 5 + [sq, sq, sq, _const_spec((1, D_MODEL)),
                                       _const_spec(w_q.shape), _const_spec(sk.shape)],
        out_specs=[tok(D_MODEL), pl.BlockSpec((D_MODEL, TM), lambda i: (0, i)),
                   pl.BlockSpec((nsk, PEER_NKEYS, TM), lambda i: (0, 0, i))],
        out_shape=[jax.ShapeDtypeStruct((n, D_MODEL), F32), jax.ShapeDtypeStruct((D_MODEL, n), BF16),
                   jax.ShapeDtypeStruct((nsk, PEER_NKEYS, n), F32)],
        compiler_params=_params(("parallel",)),
        name="merge",
    )(x2, hg, attn, ga, gb, w_a_out, w_b_out, w_o, g_ffn, w_q, sk)


_PAIR_LIMIT = [PEER_TOPK // (i + 1) for i in range(PEER_TOPK)]
_NOT_TOP = 99.0


def _peer_select_kernel(st_ref, r2_ref, e2_ref, n1_ref, e1_ref):
    ninf = -jnp.inf
    sub = 8
    row8 = lax.broadcasted_iota(I32, (sub, PEER_TM), 0)

    def top(s):
        vals, cur = [], s
        rank = jnp.full(s.shape, _NOT_TOP, F32)
        for k in range(PEER_TOPK):
            m = jnp.max(cur, axis=0, keepdims=True)
            vals.append(m)
            sel = cur == m
            rank = jnp.where(sel, float(k), rank)
            cur = jnp.where(sel, ninf, cur)
        return vals, rank

    def head(h, carry):
        s1 = st_ref[2 * h]
        s2 = st_ref[2 * h + 1]
        v1, r1 = top(s1)
        v2, r2 = top(s2)
        v2t = jnp.concatenate(v2, axis=0)
        v1_low = jnp.concatenate(v1[sub:], axis=0)
        groups = [v1[0] + v2t]
        for i in range(1, sub):
            groups.append(jnp.where(row8 < _PAIR_LIMIT[i], v1[i] + v2t[:sub], ninf))
        groups.append(v1_low + v2[0])
        cand = jnp.concatenate(groups, axis=0)
        cur = cand
        for r in range(PEER_TOPK):
            thr = jnp.max(cur, axis=0, keepdims=True)
            if r + 1 < PEER_TOPK:
                cur = jnp.where(cur == thr, ninf, cur)
        picked = cand >= thr
        z = jnp.sum(jnp.where(picked, jnp.exp(cand - (v1[0] + v2[0])), 0.0), axis=0, keepdims=True)
        cnt = jnp.where(picked, 1.0, 0.0)
        n_of_rank = [jnp.sum(cnt[:PEER_TOPK], axis=0, keepdims=True)]
        for i in range(1, sub):
            lo = PEER_TOPK + (i - 1) * sub
            n_of_rank.append(jnp.sum(cnt[lo:lo + sub], axis=0, keepdims=True))
        lo = PEER_TOPK + (sub - 1) * sub
        for i in range(sub, PEER_TOPK):
            n_of_rank.append(cnt[lo + i - sub:lo + i - sub + 1])
        n1 = jnp.zeros(s1.shape, F32)
        for i in range(PEER_TOPK):
            n1 = jnp.where(r1 == float(i), n_of_rank[i], n1)
        r2_ref[h] = r2.astype(BF16)
        e2_ref[h] = (jnp.exp(s2 - v2[0]) / z).astype(BF16)
        n1_ref[h] = n1
        e1_ref[h] = jnp.exp(s1 - v1[0])
        return carry

    lax.fori_loop(0, PEER_HEADS, head, 0)


def _peer_select(st):
    n = st.shape[2]
    big = pl.BlockSpec((PEER_HEADS, PEER_NKEYS, PEER_TM), lambda i: (0, 0, i))
    sds = lambda dt: jax.ShapeDtypeStruct((PEER_HEADS, PEER_NKEYS, n), dt)
    return pl.pallas_call(
        _peer_select_kernel,
        grid=(n // PEER_TM,),
        in_specs=[pl.BlockSpec((2 * PEER_HEADS, PEER_NKEYS, PEER_TM), lambda i: (0, 0, i))],
        out_specs=[big, big, big, big],
        out_shape=[sds(BF16), sds(BF16), sds(F32), sds(F32)],
        compiler_params=_params(("parallel",)),
        name="peer_select",
    )(st)


def _gelu_folded(x):
    c = math.sqrt(2.0 / math.pi)
    half = 0.5 * x
    return half + half * jnp.tanh(x * (c + (c * 0.044715) * (x * x)))


def _peer_dense_kernel(xnt_ref, u_ref, vt_ref, r2_ref, e2_ref, n1_ref, e1_ref, o_ref, acc_s):
    e = pl.program_id(1)
    pack = 16
    n_pack = PEER_NKEYS // pack
    per_chunk = PEER_EB // PEER_NKEYS

    @pl.when(e == 0)
    def _():
        acc_s[...] = jnp.zeros(acc_s.shape, F32)

    def gate_weights(a):
        w = [jnp.zeros((pack, PEER_TM), BF16) for _ in range(n_pack)]
        for h in range(PEER_HEADS):
            n1 = jnp.broadcast_to(n1_ref[h, pl.ds(a, 1), :], (pack, PEER_TM)).astype(BF16)
            e1 = jnp.broadcast_to(e1_ref[h, pl.ds(a, 1), :], (pack, PEER_TM)).astype(BF16)
            for r in range(n_pack):
                rows = slice(r * pack, (r + 1) * pack)
                w[r] = w[r] + jnp.where(r2_ref[h, rows, :] < n1, e2_ref[h, rows, :], 0.0) * e1
        return jnp.concatenate(w, axis=0)

    xnt = xnt_ref[...]
    total = None
    for k in range(PEER_CHUNKS):
        a0 = (e * PEER_CHUNKS + k) * per_chunk
        w = [gate_weights(a0 + aa) for aa in range(per_chunk)]
        act_t = _dot(u_ref[k * PEER_EB:(k + 1) * PEER_EB, :], xnt)
        g = [w[aa] * _gelu_folded(act_t[aa * PEER_NKEYS:(aa + 1) * PEER_NKEYS, :]).astype(BF16)
             for aa in range(per_chunk)]
        part = _dot(vt_ref[:, k * PEER_EB:(k + 1) * PEER_EB], jnp.concatenate(g, axis=0))
        total = part if total is None else total + part
    acc_s[...] += total

    @pl.when(e == pl.num_programs(1) - 1)
    def _():
        o_ref[...] = acc_s[...].T


def _peer_dense(xnt, u_b, vt_b, r2, e2, n1, e1):
    n = xnt.shape[1]
    step = PEER_EB * PEER_CHUNKS
    big = pl.BlockSpec((PEER_HEADS, PEER_NKEYS, PEER_TM), lambda i, e: (0, 0, i))
    return pl.pallas_call(
        _peer_dense_kernel,
        grid=(n // PEER_TM, PEER_EXPERTS // step),
        in_specs=[pl.BlockSpec((D_MODEL, PEER_TM), lambda i, e: (0, i)),
                  pl.BlockSpec((step, D_MODEL), lambda i, e: (e, 0)),
                  pl.BlockSpec((D_MODEL, step), lambda i, e: (0, e)),
                  big, big, big, big],
        out_specs=pl.BlockSpec((PEER_TM, D_MODEL), lambda i, e: (i, 0)),
        out_shape=jax.ShapeDtypeStruct((n, D_MODEL), F32),
        scratch_shapes=[pltpu.VMEM((D_MODEL, PEER_TM), F32)],
        compiler_params=_params(("parallel", "arbitrary")),
        name="peer_dense",
    )(xnt, u_b, vt_b, r2, e2, n1, e1)


def _final_kernel(x1_ref, po_ref, p_ref, gp_ref, wg_ref, wp_ref, gfin_ref, y_ref):
    x2 = x1_ref[...] + po_ref[...]
    gate = _sigmoid(_dot(_rms(x2, gp_ref[...]).astype(BF16), wg_ref[...]))
    x3 = x2 + gate * _dot(p_ref[...].astype(BF16), wp_ref[...])
    y_ref[...] = _rms(x3, gfin_ref[...])


def _final(x1, po, p2, g_ple, w_gate, w_proj, g_final):
    n = x1.shape[0]
    tok = lambda w: pl.BlockSpec((TM, w), lambda i: (i, 0))
    vec = _const_spec((1, D_MODEL))
    return pl.pallas_call(
        _final_kernel,
        grid=(n // TM,),
        in_specs=[tok(D_MODEL), tok(D_MODEL), tok(D_PLE), vec, _const_spec((D_MODEL, D_MODEL)),
                  _const_spec((D_PLE, D_MODEL)), vec],
        out_specs=tok(D_MODEL),
        out_shape=jax.ShapeDtypeStruct((n, D_MODEL), F32),
        compiler_params=_params(("parallel",)),
        name="final",
    )(x1, po, p2, g_ple, w_gate, w_proj, g_final)


def _pad_keys(x, lpad, axis):
    pad = lpad - x.shape[axis]
    if pad == 0:
        return x
    widths = [(0, 0)] * x.ndim
    widths[axis] = (0, pad)
    return jnp.pad(x, widths)


def _layer(x, p, conv_state, h_state, cache_k, cache_v, cache_ki, w):
    b, t, _ = x.shape
    n = b * t
    kvd = N_KV_HEADS * HEAD_DIM
    (rx, rgate, q, k, v, kb, vb, qi, ga, gb, ki, kib, wi, *key_major) = _inproj(
        x.reshape(n, D_MODEL), w["g_mix"], w["w_in"], t)

    hg, conv_new, h_last = _rglru(rx.reshape(b, t, D_RNN), rgate.reshape(b, t, D_RNN), conv_state,
                                  h_state.reshape(b, 1, D_RNN), w["conv_w"], w["conv_b"], w["w_rg_a"],
                                  w["b_rg_a"], w["w_rg_x"], w["b_rg_x"], w["rg_lambda"])

    k_all, v_all, ki_all = kb.reshape(b, t, kvd), vb.reshape(b, t, kvd), kib.reshape(b, t, IDX_DIM)
    past = 0
    if cache_k is not None:
        past = cache_k.shape[1]
        k_all = jnp.concatenate([cache_k.reshape(b, past, kvd).astype(BF16), k_all], axis=1)
        v_all = jnp.concatenate([cache_v.reshape(b, past, kvd).astype(BF16), v_all], axis=1)
        ki_all = jnp.concatenate([cache_ki.astype(BF16), ki_all], axis=1)
    l_real = past + t
    assert past % DSA_BLK == 0 and t % min(t, DSA_BLK) == 0 and n % PEER_TM == 0
    lpad = -(-l_real // (2 * DSA_BLK)) * (2 * DSA_BLK)
    if past == 0 and key_major and lpad == t:
        kt, kie, kio = key_major
    else:
        kt = _pad_keys(jnp.swapaxes(k_all, 1, 2), lpad, 2)
        kit = _pad_keys(jnp.swapaxes(ki_all, 1, 2), lpad, 2)
        zeros = jnp.zeros_like(kit)
        kie = jnp.concatenate([kit, zeros], axis=1)
        kio = jnp.concatenate([zeros, kit], axis=1)
    v_all = _pad_keys(v_all, lpad, 1)
    tq = min(t, DSA_BLK)
    attn = _dsa(q.reshape(b, t, D_MODEL), qi.reshape(b, t, IDX_HEADS * IDX_DIM), wi.reshape(b, t, IDX_HEADS),
                kt, v_all, kie, kio, _near_bias(w["rel_bias"], tq),
                past=past, l_real=l_real, n_sel=min(TOPK_MAX, l_real // 4))
    x1, xnt, st = _merge(x.reshape(n, D_MODEL), hg.reshape(n, D_RNN), attn.reshape(n, D_MODEL), ga, gb,
                        w["w_a_out"], w["w_b_out"], w["w_o"], w["g_ffn"], w["w_peer_q"], w["peer_sk"])
    r2, e2, n1, e1 = _peer_select(st)
    po = _peer_dense(xnt, w["peer_u"], w["peer_vt"], r2, e2, n1, e1)
    y = _final(x1, po, p.reshape(n, D_PLE), w["g_ple"], w["w_ple_gate"], w["w_ple_proj"], w["g_final"])
    return (y.reshape(b, t, D_MODEL), k.reshape(b, t, N_KV_HEADS, HEAD_DIM),
            v.reshape(b, t, N_KV_HEADS, HEAD_DIM), ki.reshape(b, t, IDX_DIM), conv_new,
            h_last.reshape(b, D_RNN))


def kernel(x_prompt, x_sample, p_prompt, p_sample, state_conv, state_rglru, cache_k, cache_v, cache_idx_k, rel_bias, g_mix, w_in, conv_w, conv_b, w_rg_a, b_rg_a, w_rg_x, b_rg_x, rg_lambda, w_a_out, w_b_out, w_o, g_ffn, w_peer_q, peer_sub_keys, peer_u, peer_v, g_ple, w_ple_gate, w_ple_proj, g_final):
    assert g_mix.shape[0] == 1, "single trunk layer"
    row = lambda a: a.reshape(1, -1)
    wi_full = w_in[0]
    cut = _C_GA
    tail = IDX_DIM + IDX_HEADS
    w_r = jnp.concatenate([wi_full[:, :cut], wi_full[:, cut + tail:], wi_full[:, cut:cut + tail],
                           jnp.zeros((D_MODEL, _C_END - _C_TAIL - tail), F32)], axis=1).astype(BF16)
    w = dict(
        g_mix=row(g_mix[0]), w_in=w_r, conv_w=conv_w[0], conv_b=row(conv_b[0]),
        w_rg_a=w_rg_a[0].astype(BF16), b_rg_a=row(b_rg_a[0]), w_rg_x=w_rg_x[0].astype(BF16),
        b_rg_x=row(b_rg_x[0]), rg_lambda=row(rg_lambda[0]), rel_bias=rel_bias,
        w_a_out=w_a_out[0].astype(BF16), w_b_out=w_b_out[0].astype(BF16), w_o=w_o[0].astype(BF16),
        g_ffn=row(g_ffn[0]), w_peer_q=w_peer_q[0].astype(BF16),
        peer_sk=peer_sub_keys[0].reshape(2 * PEER_HEADS, PEER_NKEYS, PEER_DK // 2).astype(BF16),
        peer_u=peer_u[0].astype(BF16), peer_vt=peer_v[0].astype(BF16).T,
        g_ple=row(g_ple[0]), w_ple_gate=w_ple_gate[0].astype(BF16), w_ple_proj=w_ple_proj[0].astype(BF16),
        g_final=row(g_final),
    )
    bp = x_prompt.shape[0]
    zc = jnp.zeros((bp, CONV_W - 1, D_RNN), F32)
    zh = jnp.zeros((bp, D_RNN), F32)
    yp, k1, v1, ki1, c1, r1 = _layer(x_prompt, p_prompt[0], zc, zh, None, None, None, w)
    ys, k2, v2, ki2, c2, r2 = _layer(x_sample, p_sample[0], state_conv[0], state_rglru[0],
                                     cache_k[0], cache_v[0], cache_idx_k[0], w)
    return (yp, ys, k1[None], v1[None], ki1[None], c1[None], r1[None],
            k2[None], v2[None], ki2[None], c2[None], r2[None])
```

```python
import functools
import math

import jax
import jax.numpy as jnp
from jax import lax
from jax.experimental import pallas as pl
from jax.experimental.pallas import tpu as pltpu

F32 = jnp.float32
BF16 = jnp.bfloat16
I32 = jnp.int32
LOG2E = math.log2(math.e)

D_MODEL = 1024
CHUNK = 64
CHUNK_SHIFT = 6
D_PLE = 256
D_RNN = 1024
RG_BLOCKS = 8
RG_BLOCK = D_RNN // RG_BLOCKS
CONV_W = 4
RG_C = 8.0
N_HEADS = 8
HEAD_DIM = 128
N_KV_HEADS = 2
KV_GROUP = N_HEADS // N_KV_HEADS
IDX_HEADS = 16
IDX_DIM = 64
TOPK_MAX = 256
NUM_BUCKETS = 32
MAX_DISTANCE = 128
PEER_HEADS = 8
PEER_NKEYS = 128
PEER_EXPERTS = PEER_NKEYS * PEER_NKEYS
PEER_DK = 256
PEER_TOPK = 16
EPS = 1e-6
NEG = -1e30
INT_MIN = -(2 ** 31)
Q_SCALE = HEAD_DIM ** -0.5 * LOG2E

LANES = 128
LANE_SHIFT = 7
VMEM_LIMIT = 56 * 1024 * 1024

_C_RX, _C_RG, _C_Q, _C_K, _C_V, _C_QI, _C_GA, _C_GB, _C_TAIL, _C_END = (
    0, 1024, 2048, 3072, 3328, 3584, 4608, 5632, 6656, 6784)

TM = 512
DSA_BLK = 256
DSA_WIDE = 1024
SEARCH_ROWS = 128
PEER_TM = 512
PEER_EB = 512
PEER_CHUNKS = 4


def _params(sem):
    return pltpu.CompilerParams(dimension_semantics=sem, vmem_limit_bytes=VMEM_LIMIT)


def _const_spec(shape):
    nd = len(shape)
    return pl.BlockSpec(shape, lambda *_: (0,) * nd, pipeline_mode=pl.Buffered(1))


def _rms(x, g):
    return x * lax.rsqrt(jnp.mean(x * x, axis=-1, keepdims=True) + EPS) * g


def _gelu(x):
    return 0.5 * x * (1.0 + jnp.tanh(math.sqrt(2.0 / math.pi) * (x + 0.044715 * (x * x * x))))


def _sigmoid(x):
    return 1.0 / (1.0 + jnp.exp(-x))


def _dot(a, b):
    return jnp.dot(a, b, preferred_element_type=F32)


def _dot_nt(a, b):
    return lax.dot_general(a, b, (((1,), (1,)), ((), ())), preferred_element_type=F32)


def _inproj_kernel(x_ref, g_ref, w_ref, rx_ref, rg_ref, q_ref, k_ref, v_ref, kb_ref, vb_ref,
                   qi_ref, ga_ref, gb_ref, ki_ref, kib_ref, wi_ref, *key_major_refs):
    n = _rms(x_ref[...], g_ref[...]).astype(BF16)

    def mm(a, b):
        return _dot(n, w_ref[:, a:b])

    rx_ref[...] = mm(_C_RX, _C_RG)
    rg_ref[...] = mm(_C_RG, _C_Q)
    q_ref[...] = (mm(_C_Q, _C_K) * Q_SCALE).astype(BF16)
    k = mm(_C_K, _C_V)
    k_ref[...] = k
    kb_ref[...] = k.astype(BF16)
    v = mm(_C_V, _C_QI)
    v_ref[...] = v
    vb_ref[...] = v.astype(BF16)
    qi_ref[...] = mm(_C_QI, _C_GA).astype(BF16)
    ga_ref[...] = mm(_C_GA, _C_GB)
    gb_ref[...] = mm(_C_GB, _C_TAIL)
    tail = mm(_C_TAIL, _C_END)
    ki = tail[:, :IDX_DIM]
    ki_ref[...] = ki
    kib_ref[...] = ki.astype(BF16)
    wi_ref[...] = tail[:, IDX_DIM:IDX_DIM + IDX_HEADS]
    if key_major_refs:
        kt_ref, kie_ref, kio_ref = key_major_refs
        kt_ref[0] = k.T.astype(BF16)
        tail_t = tail.T
        kit = tail_t[:IDX_DIM].astype(BF16)
        zeros = jnp.zeros_like(kit)
        kie_ref[0] = jnp.concatenate([kit, zeros], axis=0)
        kio_ref[0] = jnp.concatenate([zeros, kit], axis=0)


def _inproj(x2, g_mix, w_r, seq_len):
    n = x2.shape[0]
    kvd = N_KV_HEADS * HEAD_DIM
    tok = lambda w: pl.BlockSpec((TM, w), lambda i: (i, 0))
    extra_specs, extra_shapes = [], []
    if seq_len % TM == 0:
        per_seq = seq_len // TM
        kmaj = lambda r: pl.BlockSpec((1, r, TM), lambda i: (i // per_seq, 0, i % per_seq))
        extra_specs = [kmaj(kvd), kmaj(LANES), kmaj(LANES)]
        extra_shapes = [jax.ShapeDtypeStruct((n // seq_len, r, seq_len), BF16) for r in (kvd, LANES, LANES)]
    widths_dtypes = [(D_RNN, F32), (D_RNN, F32), (D_MODEL, BF16), (kvd, F32), (kvd, F32), (kvd, BF16),
                     (kvd, BF16), (IDX_HEADS * IDX_DIM, BF16), (D_MODEL, F32), (D_MODEL, F32),
                     (IDX_DIM, F32), (IDX_DIM, BF16), (IDX_HEADS, F32)]
    return pl.pallas_call(
        _inproj_kernel,
        grid=(n // TM,),
        in_specs=[tok(D_MODEL), _const_spec((1, D_MODEL)), _const_spec(w_r.shape)],
        out_specs=[tok(w) for w, _ in widths_dtypes] + extra_specs,
        out_shape=[jax.ShapeDtypeStruct((n, w), dt) for w, dt in widths_dtypes] + extra_shapes,
        compiler_params=_params(("parallel",)),
        name="inproj",
    )(x2, g_mix, w_r)


def _rglru_kernel(x_ref, gate_ref, cs_ref, h0_ref, cw_ref, cb_ref, wa_ref, ba_ref, wx_ref, bx_ref,
                  lam_ref, hg_ref, cnew_ref, hlast_ref, xp_s, hc_s, *, tt):
    @pl.when(pl.program_id(1) == 0)
    def _():
        xp_s[5:8, :] = cs_ref[0]
        hc_s[...] = h0_ref[0]

    xp_s[8:8 + tt, :] = x_ref[0]
    nl = -lam_ref[...]
    softplus = jnp.maximum(nl, 0.0) + jnp.log1p(jnp.exp(-jnp.abs(nl)))
    row = lax.broadcasted_iota(I32, (tt, RG_BLOCK), 0)
    for n in range(RG_BLOCKS):
        sl = slice(n * RG_BLOCK, (n + 1) * RG_BLOCK)
        xc = cb_ref[:, sl]
        for j in range(CONV_W):
            xc = xc + cw_ref[j:j + 1, sl] * xp_s[5 + j:5 + j + tt, sl]
        xcb = xc.astype(BF16)
        r = _sigmoid(_dot(xcb, wa_ref[n]) + ba_ref[:, sl])
        ig = _sigmoid(_dot(xcb, wx_ref[n]) + bx_ref[:, sl])
        log_a = -RG_C * r * softplus[:, sl]
        a = jnp.exp(log_a)
        mult = jnp.sqrt(-jnp.tanh(log_a) * (a * a + 1.0))
        bv = mult * (ig * xc)
        d = 1
        while d < tt:
            keep = row >= d
            a_sh = pltpu.roll(a, d, 0)
            b_sh = pltpu.roll(bv, d, 0)
            bv = jnp.where(keep, a * b_sh + bv, bv)
            a = jnp.where(keep, a * a_sh, a)
            d *= 2
        h = a * hc_s[:, sl] + bv
        hc_s[:, sl] = h[tt - 1:tt, :]
        hg_ref[0, :, sl] = (h * _gelu(gate_ref[0, :, sl])).astype(BF16)
    tail = xp_s[tt + 5:tt + 8, :]
    cnew_ref[0] = tail
    xp_s[5:8, :] = tail
    hlast_ref[0] = hc_s[...]


def _rglru(rx, rgate, conv_state, h0, conv_w, conv_b, w_a, b_a, w_x, b_x, lam):
    b, t, _ = rx.shape
    tt = min(t, 256)
    seq = pl.BlockSpec((1, tt, D_RNN), lambda i, j: (i, j, 0))
    per_b = lambda r: pl.BlockSpec((1, r, D_RNN), lambda i, j: (i, 0, 0))
    vec = _const_spec((1, D_RNN))
    wblk = _const_spec((RG_BLOCKS, RG_BLOCK, RG_BLOCK))
    return pl.pallas_call(
        functools.partial(_rglru_kernel, tt=tt),
        grid=(b, t // tt),
        in_specs=[seq, seq, per_b(CONV_W - 1), per_b(1), _const_spec((CONV_W, D_RNN)), vec,
                  wblk, vec, wblk, vec, vec],
        out_specs=[seq, per_b(CONV_W - 1), per_b(1)],
        out_shape=[jax.ShapeDtypeStruct((b, t, D_RNN), BF16),
                   jax.ShapeDtypeStruct((b, CONV_W - 1, D_RNN), F32),
                   jax.ShapeDtypeStruct((b, 1, D_RNN), F32)],
        scratch_shapes=[pltpu.VMEM((tt + 8, D_RNN), F32), pltpu.VMEM((1, D_RNN), F32)],
        compiler_params=_params(("parallel", "arbitrary")),
        name="rglru",
    )(rx, rgate, conv_state, h0, conv_w, conv_b, w_a, b_a, w_x, b_x, lam)


def _dsa_kernel(q_ref, qi_ref, wi_ref, kt_ref, v_ref, kie_ref, kio_ref, nb_ref, o_ref,
                key_s, wb_s, qs_s, m_s, l_s, acc_s, *, tq, past, l_real, n_sel):
    kb = DSA_BLK
    nc = kb // LANES
    i = pl.program_id(1)
    q0 = past + i * tq
    own = q0 // kb
    nkb = own + 1
    n_unit = (nkb + 1) // 2

    def lanes_at(k0):
        return lax.shift_right_logical(k0, LANE_SHIFT)

    wi = wi_ref[0] * (IDX_HEADS ** -0.5 * IDX_DIM ** -0.5)
    for h in range(IDX_HEADS):
        wb_s[h] = jnp.broadcast_to(wi[:, h:h + 1], (tq, LANES))
    q_chunk = (q0 + lax.broadcasted_iota(I32, (tq, LANES), 0)) >> CHUNK_SHIFT
    lane = lax.broadcasted_iota(I32, (tq, LANES), 1)

    def score_block(j, carry):
        k0 = pl.multiple_of(j * kb, kb)
        acc = [jnp.zeros((tq, LANES), F32) for _ in range(nc)]
        for h2 in range(IDX_HEADS // 2):
            qpair = qi_ref[0, :, h2 * LANES:(h2 + 1) * LANES]
            for par, kref in ((0, kie_ref), (1, kio_ref)):
                s = _dot(qpair, kref[0, :, pl.ds(k0, kb)])
                w = wb_s[2 * h2 + par]
                for c in range(nc):
                    acc[c] = acc[c] + jnp.maximum(s[:, c * LANES:(c + 1) * LANES], 0.0) * w
        for c in range(nc):
            bits = pltpu.bitcast(acc[c], I32)
            key = bits ^ ((bits >> 31) & 0x7FFFFFFF)
            kpos = k0 + c * LANES + lane
            key = jnp.where((kpos >> CHUNK_SHIFT) <= q_chunk, key, INT_MIN)
            key = jnp.where(kpos < l_real, key, INT_MIN)
            key_s[lanes_at(k0 + c * LANES)] = key
        return carry

    lax.fori_loop(0, nkb, score_block, 0)

    @pl.when(nkb % 2 == 1)
    def _():
        for c in range(nc):
            key_s[lanes_at(nkb * kb + c * LANES)] = jnp.full((tq, LANES), INT_MIN, I32)

    unit = 2 * kb
    strip = min(tq, SEARCH_ROWS)
    n_strip = tq // strip

    def search_strip(s):
        rows = slice(s * strip, (s + 1) * strip)

        def count_ge(cand):
            def body(u, cnt):
                for c in range(unit // LANES):
                    blk = key_s[lanes_at(u * unit + c * LANES), rows, :]
                    cnt = cnt + jnp.where(blk >= cand, 1.0, 0.0)
                return cnt

            cnt = lax.fori_loop(0, n_unit, body, jnp.zeros((strip, LANES), F32))
            return jnp.broadcast_to(jnp.sum(cnt, axis=1, keepdims=True), (strip, LANES))

        n_adm = jnp.minimum((q_chunk[rows] + 1) * CHUNK, l_real).astype(F32)

        def step(carry):
            it, res, cnt, _ = carry
            cand = jnp.where(it == 0, 0, res | (1 << (31 - it)))
            c = count_ge(cand)
            take = c >= n_sel
            res = jnp.where(take, cand, res)
            cnt = jnp.where(take, c, cnt)
            return it + 1, res, cnt, jnp.max(cnt) > n_sel

        init = (jnp.int32(0), jnp.full((strip, LANES), INT_MIN, I32), n_adm, jnp.max(n_adm) > n_sel)
        return lax.while_loop(lambda c: (c[0] < 32) & c[3], step, init)[1]

    res = [search_strip(s) for s in range(n_strip)]
    thr = jnp.maximum(jnp.concatenate(res, axis=0), INT_MIN + 1)

    for g in range(N_KV_HEADS):
        for hh in range(KV_GROUP):
            h = g * KV_GROUP + hh
            qs_s[g, hh * tq:(hh + 1) * tq, :] = q_ref[0, :, h * HEAD_DIM:(h + 1) * HEAD_DIM]
    m_s[...] = jnp.full(m_s.shape, NEG, F32)
    l_s[...] = jnp.zeros(l_s.shape, F32)
    acc_s[...] = jnp.zeros(acc_s.shape, F32)

    def attend(k0, width, near):
        ncw = width // LANES
        mb = [jnp.where(key_s[lanes_at(k0 + c * LANES)] >= thr, 0.0, NEG) for c in range(ncw)]
        for g in range(N_KV_HEADS):
            s_all = _dot(qs_s[g], kt_ref[0, g * HEAD_DIM:(g + 1) * HEAD_DIM, pl.ds(k0, width)])
            vb = v_ref[0, pl.ds(k0, width), g * HEAD_DIM:(g + 1) * HEAD_DIM]
            for hh in range(KV_GROUP):
                h = g * KV_GROUP + hh
                s = s_all[hh * tq:(hh + 1) * tq, :]
                sc = []
                for c in range(ncw):
                    x = s[:, c * LANES:(c + 1) * LANES] + mb[c]
                    if near is not None:
                        x = x + nb_ref[near, h, :, c * LANES:(c + 1) * LANES]
                    sc.append(x)
                mx = sc[0]
                for c in range(1, ncw):
                    mx = jnp.maximum(mx, sc[c])
                m_old = m_s[h]
                m_new = jnp.maximum(m_old, jnp.broadcast_to(jnp.max(mx, axis=1, keepdims=True), (tq, LANES)))
                alpha = jnp.exp2(m_old - m_new)
                p = [jnp.exp2(x - m_new) for x in sc]
                ps = p[0]
                for c in range(1, ncw):
                    ps = ps + p[c]
                l_s[h] = alpha * l_s[h] + jnp.broadcast_to(jnp.sum(ps, axis=1, keepdims=True), (tq, LANES))
                m_s[h] = m_new
                pb = jnp.concatenate(p, axis=1).astype(BF16)
                acc_s[h] = alpha * acc_s[h] + _dot(pb, vb)

    n_far = jnp.maximum(own - 1, 0)
    per_wide = DSA_WIDE // kb
    n_wide = n_far // per_wide

    def wide_step(j, carry):
        attend(pl.multiple_of(j * DSA_WIDE, DSA_WIDE), DSA_WIDE, None)
        return carry

    def far_block(j, carry):
        attend(pl.multiple_of(j * kb, kb), kb, None)
        return carry

    lax.fori_loop(0, n_wide, wide_step, 0)
    lax.fori_loop(n_wide * per_wide, n_far, far_block, 0)

    @pl.when(own >= 1)
    def _():
        attend(pl.multiple_of((own - 1) * kb, kb), kb, 0)

    attend(pl.multiple_of(own * kb, kb), kb, 1)

    for h in range(N_HEADS):
        o_ref[0, :, h * HEAD_DIM:(h + 1) * HEAD_DIM] = (acc_s[h] / l_s[h]).astype(BF16)


def _t5_bucket(rel):
    nb = NUM_BUCKETS // 2
    max_exact = nb // 2
    ret = jnp.where(rel > 0, nb, 0)
    n = jnp.abs(rel)
    n_f = jnp.maximum(n, 1).astype(F32)
    large = max_exact + (jnp.log(n_f / max_exact) / math.log(MAX_DISTANCE / max_exact)
                         * (nb - max_exact)).astype(I32)
    large = jnp.minimum(large, nb - 1)
    return ret + jnp.where(n < max_exact, n, large)


def _near_bias(rel_bias, tq):
    a = jnp.arange(tq, dtype=I32)[:, None]
    b = jnp.arange(DSA_BLK, dtype=I32)[None, :]
    centered = (rel_bias - rel_bias[NUM_BUCKETS // 2 - 1]) * LOG2E
    tiles = []
    for d in (-DSA_BLK, 0):
        bucket = jnp.where(d + b - a > -MAX_DISTANCE, _t5_bucket(d + b - a), NUM_BUCKETS // 2 - 1)
        t = jnp.zeros((N_HEADS, tq, DSA_BLK), F32)
        for c in range(NUM_BUCKETS):
            t = jnp.where((bucket == c)[None], centered[c][:, None, None], t)
        tiles.append(t)
    return jnp.stack(tiles)


def _dsa(q, qi, wi, kt, v, kie, kio, nbias, *, past, l_real, n_sel):
    b, t, _ = q.shape
    tq = min(t, DSA_BLK)
    lpad = kt.shape[2]
    tile = lambda w: pl.BlockSpec((1, tq, w), lambda i, j: (i, j, 0))
    per_b = lambda r, c: pl.BlockSpec((1, r, c), lambda i, j: (i, 0, 0), pipeline_mode=pl.Buffered(1))
    kvd = N_KV_HEADS * HEAD_DIM
    return pl.pallas_call(
        functools.partial(_dsa_kernel, tq=tq, past=past, l_real=l_real, n_sel=n_sel),
        grid=(b, t // tq),
        in_specs=[tile(D_MODEL), tile(IDX_HEADS * IDX_DIM), tile(IDX_HEADS),
                  per_b(kvd, lpad), per_b(lpad, kvd), per_b(LANES, lpad), per_b(LANES, lpad),
                  _const_spec(nbias.shape)],
        out_specs=tile(D_MODEL),
        out_shape=jax.ShapeDtypeStruct((b, t, D_MODEL), BF16),
        scratch_shapes=[pltpu.VMEM((lpad // LANES, tq, LANES), I32),
                        pltpu.VMEM((IDX_HEADS, tq, LANES), F32),
                        pltpu.VMEM((N_KV_HEADS, KV_GROUP * tq, HEAD_DIM), BF16),
                        pltpu.VMEM((N_HEADS, tq, LANES), F32),
                        pltpu.VMEM((N_HEADS, tq, LANES), F32),
                        pltpu.VMEM((N_HEADS, tq, HEAD_DIM), F32)],
        compiler_params=_params(("parallel", "arbitrary")),
        name="dsa",
    )(q, qi, wi, kt, v, kie, kio, nbias)


def _merge_kernel(x_ref, hg_ref, at_ref, ga_ref, gb_ref, wa_ref, wb_ref, wo_ref, gf_ref, wq_ref, sk_ref,
                  x1_ref, xnt_ref, st_ref):
    ya = _dot(hg_ref[...], wa_ref[...])
    yb = _dot(at_ref[...], wb_ref[...])
    m = _sigmoid(ga_ref[...]) * ya + _sigmoid(gb_ref[...]) * yb
    x1 = x_ref[...] + _dot(m.astype(BF16), wo_ref[...])
    x1_ref[...] = x1
    xn = _rms(x1, gf_ref[...])
    xnt_ref[...] = xn.T.astype(BF16)
    qp = _dot(xn.astype(BF16), wq_ref[...])
    for j in range(2 * PEER_HEADS):
        qj = qp[:, j * LANES:(j + 1) * LANES].astype(BF16)
        st_ref[j] = _dot_nt(sk_ref[j], qj)


def _merge(x2, hg, attn, ga, gb, w_a_out, w_b_out, w_o, g_ffn, w_q, sk):
    n = x2.shape[0]
    tok = lambda w: pl.BlockSpec((TM, w), lambda i: (i, 0))
    sq = _const_spec((D_MODEL, D_MODEL))
    nsk = 2 * PEER_HEADS
    return pl.pallas_call(
        _merge_kernel,
        grid=(n // TM,),
        in_specs=[tok(D_MODEL)] * 5 + [sq, sq, sq, _const_spec((1, D_MODEL)),
                                       _const_spec(w_q.shape), _const_spec(sk.shape)],
        out_specs=[tok(D_MODEL), pl.BlockSpec((D_MODEL, TM), lambda i: (0, i)),
                   pl.BlockSpec((nsk, PEER_NKEYS, TM), lambda i: (0, 0, i))],
        out_shape=[jax.ShapeDtypeStruct((n, D_MODEL), F32), jax.ShapeDtypeStruct((D_MODEL, n), BF16),
                   jax.ShapeDtypeStruct((nsk, PEER_NKEYS, n), F32)],
        compiler_params=_params(("parallel",)),
        name="merge",
    )(x2, hg, attn, ga, gb, w_a_out, w_b_out, w_o, g_ffn, w_q, sk)


_PAIR_LIMIT = [PEER_TOPK // (i + 1) for i in range(PEER_TOPK)]
_NOT_TOP = 99.0


def _peer_select_kernel(st_ref, r2_ref, e2_ref, n1_ref, e1_ref):
    ninf = -jnp.inf
    sub = 8
    row8 = lax.broadcasted_iota(I32, (sub, PEER_TM), 0)

    def top(s):
        vals, cur = [], s
        rank = jnp.full(s.shape, _NOT_TOP, F32)
        for k in range(PEER_TOPK):
            m = jnp.max(cur, axis=0, keepdims=True)
            vals.append(m)
            sel = cur == m
            rank = jnp.where(sel, float(k), rank)
            cur = jnp.where(sel, ninf, cur)
        return vals, rank

    def head(h, carry):
        s1 = st_ref[2 * h]
        s2 = st_ref[2 * h + 1]
        v1, r1 = top(s1)
        v2, r2 = top(s2)
        v2t = jnp.concatenate(v2, axis=0)
        v1_low = jnp.concatenate(v1[sub:], axis=0)
        groups = [v1[0] + v2t]
        for i in range(1, sub):
            groups.append(jnp.where(row8 < _PAIR_LIMIT[i], v1[i] + v2t[:sub], ninf))
        groups.append(v1_low + v2[0])
        cand = jnp.concatenate(groups, axis=0)
        cur = cand
        for r in range(PEER_TOPK):
            thr = jnp.max(cur, axis=0, keepdims=True)
            if r + 1 < PEER_TOPK:
                cur = jnp.where(cur == thr, ninf, cur)
        picked = cand >= thr
        z = jnp.sum(jnp.where(picked, jnp.exp(cand - (v1[0] + v2[0])), 0.0), axis=0, keepdims=True)
        cnt = jnp.where(picked, 1.0, 0.0)
        n_of_rank = [jnp.sum(cnt[:PEER_TOPK], axis=0, keepdims=True)]
        for i in range(1, sub):
            lo = PEER_TOPK + (i - 1) * sub
            n_of_rank.append(jnp.sum(cnt[lo:lo + sub], axis=0, keepdims=True))
        lo = PEER_TOPK + (sub - 1) * sub
        for i in range(sub, PEER_TOPK):
            n_of_rank.append(cnt[lo + i - sub:lo + i - sub + 1])
        n1 = jnp.zeros(s1.shape, F32)
        for i in range(PEER_TOPK):
            n1 = jnp.where(r1 == float(i), n_of_rank[i], n1)
        r2_ref[h] = r2.astype(BF16)
        e2_ref[h] = (jnp.exp(s2 - v2[0]) / z).astype(BF16)
        n1_ref[h] = n1
        e1_ref[h] = jnp.exp(s1 - v1[0])
        return carry

    lax.fori_loop(0, PEER_HEADS, head, 0)


def _peer_select(st):
    n = st.shape[2]
    big = pl.BlockSpec((PEER_HEADS, PEER_NKEYS, PEER_TM), lambda i: (0, 0, i))
    sds = lambda dt: jax.ShapeDtypeStruct((PEER_HEADS, PEER_NKEYS, n), dt)
    return pl.pallas_call(
        _peer_select_kernel,
        grid=(n // PEER_TM,),
        in_specs=[pl.BlockSpec((2 * PEER_HEADS, PEER_NKEYS, PEER_TM), lambda i: (0, 0, i))],
        out_specs=[big, big, big, big],
        out_shape=[sds(BF16), sds(BF16), sds(F32), sds(F32)],
        compiler_params=_params(("parallel",)),
        name="peer_select",
    )(st)


def _gelu_folded(x):
    k = 2.0 * math.sqrt(2.0 / math.pi) * LOG2E
    return x / (1.0 + jnp.exp2(x * (-k - (k * 0.044715) * (x * x))))


def _peer_dense_kernel(xnt_ref, u_ref, vt_ref, r2_ref, e2_ref, n1_ref, e1_ref, o_ref, acc_s):
    e = pl.program_id(1)
    pack = 16
    n_pack = PEER_NKEYS // pack
    per_chunk = PEER_EB // PEER_NKEYS

    @pl.when(e == 0)
    def _():
        acc_s[...] = jnp.zeros(acc_s.shape, F32)

    def gate_weights(a):
        w = [jnp.zeros((pack, PEER_TM), BF16) for _ in range(n_pack)]
        for h in range(PEER_HEADS):
            n1 = jnp.broadcast_to(n1_ref[h, pl.ds(a, 1), :], (pack, PEER_TM)).astype(BF16)
            e1 = jnp.broadcast_to(e1_ref[h, pl.ds(a, 1), :], (pack, PEER_TM)).astype(BF16)
            for r in range(n_pack):
                rows = slice(r * pack, (r + 1) * pack)
                w[r] = w[r] + jnp.where(r2_ref[h, rows, :] < n1, e2_ref[h, rows, :], 0.0) * e1
        return jnp.concatenate(w, axis=0)

    xnt = xnt_ref[...]
    g = []
    for k in range(PEER_CHUNKS):
        a0 = (e * PEER_CHUNKS + k) * per_chunk
        w = [gate_weights(a0 + aa) for aa in range(per_chunk)]
        act_t = _dot(u_ref[k * PEER_EB:(k + 1) * PEER_EB, :], xnt)
        g += [w[aa] * _gelu_folded(act_t[aa * PEER_NKEYS:(aa + 1) * PEER_NKEYS, :]).astype(BF16)
              for aa in range(per_chunk)]
    acc_s[...] += _dot(vt_ref[...], jnp.concatenate(g, axis=0))

    @pl.when(e == pl.num_programs(1) - 1)
    def _():
        o_ref[...] = acc_s[...].T


def _peer_dense(xnt, u_b, vt_b, r2, e2, n1, e1):
    n = xnt.shape[1]
    step = PEER_EB * PEER_CHUNKS
    big = pl.BlockSpec((PEER_HEADS, PEER_NKEYS, PEER_TM), lambda i, e: (0, 0, i))
    return pl.pallas_call(
        _peer_dense_kernel,
        grid=(n // PEER_TM, PEER_EXPERTS // step),
        in_specs=[pl.BlockSpec((D_MODEL, PEER_TM), lambda i, e: (0, i)),
                  pl.BlockSpec((step, D_MODEL), lambda i, e: (e, 0)),
                  pl.BlockSpec((D_MODEL, step), lambda i, e: (0, e)),
                  big, big, big, big],
        out_specs=pl.BlockSpec((PEER_TM, D_MODEL), lambda i, e: (i, 0)),
        out_shape=jax.ShapeDtypeStruct((n, D_MODEL), F32),
        scratch_shapes=[pltpu.VMEM((D_MODEL, PEER_TM), F32)],
        compiler_params=_params(("parallel", "arbitrary")),
        name="peer_dense",
    )(xnt, u_b, vt_b, r2, e2, n1, e1)


def _final_kernel(x1_ref, po_ref, p_ref, gp_ref, wg_ref, wp_ref, gfin_ref, y_ref):
    x2 = x1_ref[...] + po_ref[...]
    gate = _sigmoid(_dot(_rms(x2, gp_ref[...]).astype(BF16), wg_ref[...]))
    x3 = x2 + gate * _dot(p_ref[...].astype(BF16), wp_ref[...])
    y_ref[...] = _rms(x3, gfin_ref[...])


def _final(x1, po, p2, g_ple, w_gate, w_proj, g_final):
    n = x1.shape[0]
    tok = lambda w: pl.BlockSpec((TM, w), lambda i: (i, 0))
    vec = _const_spec((1, D_MODEL))
    return pl.pallas_call(
        _final_kernel,
        grid=(n // TM,),
        in_specs=[tok(D_MODEL), tok(D_MODEL), tok(D_PLE), vec, _const_spec((D_MODEL, D_MODEL)),
                  _const_spec((D_PLE, D_MODEL)), vec],
        out_specs=tok(D_MODEL),
        out_shape=jax.ShapeDtypeStruct((n, D_MODEL), F32),
        compiler_params=_params(("parallel",)),
        name="final",
    )(x1, po, p2, g_ple, w_gate, w_proj, g_final)


def _pad_keys(x, lpad, axis):
    pad = lpad - x.shape[axis]
    if pad == 0:
        return x
    widths = [(0, 0)] * x.ndim
    widths[axis] = (0, pad)
    return jnp.pad(x, widths)


def _layer(x, p, conv_state, h_state, cache_k, cache_v, cache_ki, w):
    b, t, _ = x.shape
    n = b * t
    kvd = N_KV_HEADS * HEAD_DIM
    (rx, rgate, q, k, v, kb, vb, qi, ga, gb, ki, kib, wi, *key_major) = _inproj(
        x.reshape(n, D_MODEL), w["g_mix"], w["w_in"], t)

    hg, conv_new, h_last = _rglru(rx.reshape(b, t, D_RNN), rgate.reshape(b, t, D_RNN), conv_state,
                                  h_state.reshape(b, 1, D_RNN), w["conv_w"], w["conv_b"], w["w_rg_a"],
                                  w["b_rg_a"], w["w_rg_x"], w["b_rg_x"], w["rg_lambda"])

    k_all, v_all, ki_all = kb.reshape(b, t, kvd), vb.reshape(b, t, kvd), kib.reshape(b, t, IDX_DIM)
    past = 0
    if cache_k is not None:
        past = cache_k.shape[1]
        k_all = jnp.concatenate([cache_k.reshape(b, past, kvd).astype(BF16), k_all], axis=1)
        v_all = jnp.concatenate([cache_v.reshape(b, past, kvd).astype(BF16), v_all], axis=1)
        ki_all = jnp.concatenate([cache_ki.astype(BF16), ki_all], axis=1)
    l_real = past + t
    assert past % DSA_BLK == 0 and t % min(t, DSA_BLK) == 0 and n % PEER_TM == 0
    lpad = -(-l_real // (2 * DSA_BLK)) * (2 * DSA_BLK)
    if past == 0 and key_major and lpad == t:
        kt, kie, kio = key_major
    else:
        kt = _pad_keys(jnp.swapaxes(k_all, 1, 2), lpad, 2)
        kit = _pad_keys(jnp.swapaxes(ki_all, 1, 2), lpad, 2)
        zeros = jnp.zeros_like(kit)
        kie = jnp.concatenate([kit, zeros], axis=1)
        kio = jnp.concatenate([zeros, kit], axis=1)
    v_all = _pad_keys(v_all, lpad, 1)
    tq = min(t, DSA_BLK)
    attn = _dsa(q.reshape(b, t, D_MODEL), qi.reshape(b, t, IDX_HEADS * IDX_DIM), wi.reshape(b, t, IDX_HEADS),
                kt, v_all, kie, kio, _near_bias(w["rel_bias"], tq),
                past=past, l_real=l_real, n_sel=min(TOPK_MAX, l_real // 4))
    x1, xnt, st = _merge(x.reshape(n, D_MODEL), hg.reshape(n, D_RNN), attn.reshape(n, D_MODEL), ga, gb,
                        w["w_a_out"], w["w_b_out"], w["w_o"], w["g_ffn"], w["w_peer_q"], w["peer_sk"])
    r2, e2, n1, e1 = _peer_select(st)
    po = _peer_dense(xnt, w["peer_u"], w["peer_vt"], r2, e2, n1, e1)
    y = _final(x1, po, p.reshape(n, D_PLE), w["g_ple"], w["w_ple_gate"], w["w_ple_proj"], w["g_final"])
    return (y.reshape(b, t, D_MODEL), k.reshape(b, t, N_KV_HEADS, HEAD_DIM),
            v.reshape(b, t, N_KV_HEADS, HEAD_DIM), ki.reshape(b, t, IDX_DIM), conv_new,
            h_last.reshape(b, D_RNN))


def kernel(x_prompt, x_sample, p_prompt, p_sample, state_conv, state_rglru, cache_k, cache_v, cache_idx_k, rel_bias, g_mix, w_in, conv_w, conv_b, w_rg_a, b_rg_a, w_rg_x, b_rg_x, rg_lambda, w_a_out, w_b_out, w_o, g_ffn, w_peer_q, peer_sub_keys, peer_u, peer_v, g_ple, w_ple_gate, w_ple_proj, g_final):
    assert g_mix.shape[0] == 1, "single trunk layer"
    row = lambda a: a.reshape(1, -1)
    wi_full = w_in[0]
    cut = _C_GA
    tail = IDX_DIM + IDX_HEADS
    w_r = jnp.concatenate([wi_full[:, :cut], wi_full[:, cut + tail:], wi_full[:, cut:cut + tail],
                           jnp.zeros((D_MODEL, _C_END - _C_TAIL - tail), F32)], axis=1).astype(BF16)
    w = dict(
        g_mix=row(g_mix[0]), w_in=w_r, conv_w=conv_w[0], conv_b=row(conv_b[0]),
        w_rg_a=w_rg_a[0].astype(BF16), b_rg_a=row(b_rg_a[0]), w_rg_x=w_rg_x[0].astype(BF16),
        b_rg_x=row(b_rg_x[0]), rg_lambda=row(rg_lambda[0]), rel_bias=rel_bias,
        w_a_out=w_a_out[0].astype(BF16), w_b_out=w_b_out[0].astype(BF16), w_o=w_o[0].astype(BF16),
        g_ffn=row(g_ffn[0]), w_peer_q=w_peer_q[0].astype(BF16),
        peer_sk=peer_sub_keys[0].reshape(2 * PEER_HEADS, PEER_NKEYS, PEER_DK // 2).astype(BF16),
        peer_u=peer_u[0].astype(BF16), peer_vt=peer_v[0].astype(BF16).T,
        g_ple=row(g_ple[0]), w_ple_gate=w_ple_gate[0].astype(BF16), w_ple_proj=w_ple_proj[0].astype(BF16),
        g_final=row(g_final),
    )
    bp = x_prompt.shape[0]
    zc = jnp.zeros((bp, CONV_W - 1, D_RNN), F32)
    zh = jnp.zeros((bp, D_RNN), F32)
    yp, k1, v1, ki1, c1, r1 = _layer(x_prompt, p_prompt[0], zc, zh, None, None, None, w)
    ys, k2, v2, ki2, c2, r2 = _layer(x_sample, p_sample[0], state_conv[0], state_rglru[0],
                                     cache_k[0], cache_v[0], cache_idx_k[0], w)
    return (yp, ys, k1[None], v1[None], ki1[None], c1[None], r1[None],
            k2[None], v2[None], ki2[None], c2[None], r2[None])
```

```python
import functools
import math

import jax
import jax.numpy as jnp
from jax import lax
from jax.experimental import pallas as pl
from jax.experimental.pallas import tpu as pltpu

F32 = jnp.float32
BF16 = jnp.bfloat16
I32 = jnp.int32
LOG2E = math.log2(math.e)

D_MODEL = 1024
CHUNK = 64
CHUNK_SHIFT = 6
D_PLE = 256
D_RNN = 1024
RG_BLOCKS = 8
RG_BLOCK = D_RNN // RG_BLOCKS
CONV_W = 4
RG_C = 8.0
N_HEADS = 8
HEAD_DIM = 128
N_KV_HEADS = 2
KV_GROUP = N_HEADS // N_KV_HEADS
IDX_HEADS = 16
IDX_DIM = 64
TOPK_MAX = 256
NUM_BUCKETS = 32
MAX_DISTANCE = 128
PEER_HEADS = 8
PEER_NKEYS = 128
PEER_EXPERTS = PEER_NKEYS * PEER_NKEYS
PEER_DK = 256
PEER_TOPK = 16
EPS = 1e-6
NEG = -1e30
INT_MIN = -(2 ** 31)
Q_SCALE = HEAD_DIM ** -0.5 * LOG2E

LANES = 128
LANE_SHIFT = 7
VMEM_LIMIT = 56 * 1024 * 1024

_C_RX, _C_RG, _C_Q, _C_K, _C_V, _C_QI, _C_GA, _C_GB, _C_TAIL, _C_END = (
    0, 1024, 2048, 3072, 3328, 3584, 4608, 5632, 6656, 6784)

TM = 512
DSA_BLK = 256
DSA_WIDE = 1024
SEARCH_ROWS = 128
PEER_TM = 512
PEER_CHUNK_ROWS = (128, 128, 256, 512, 1024)
PEER_STEP = sum(PEER_CHUNK_ROWS)


def _params(sem):
    return pltpu.CompilerParams(dimension_semantics=sem, vmem_limit_bytes=VMEM_LIMIT)


def _const_spec(shape):
    nd = len(shape)
    return pl.BlockSpec(shape, lambda *_: (0,) * nd, pipeline_mode=pl.Buffered(1))


def _rms(x, g):
    return x * lax.rsqrt(jnp.mean(x * x, axis=-1, keepdims=True) + EPS) * g


def _gelu(x):
    return 0.5 * x * (1.0 + jnp.tanh(math.sqrt(2.0 / math.pi) * (x + 0.044715 * (x * x * x))))


def _sigmoid(x):
    return 1.0 / (1.0 + jnp.exp(-x))


def _dot(a, b):
    return jnp.dot(a, b, preferred_element_type=F32)


def _dot_nt(a, b):
    return lax.dot_general(a, b, (((1,), (1,)), ((), ())), preferred_element_type=F32)


def _inproj_kernel(x_ref, g_ref, w_ref, rx_ref, rg_ref, q_ref, k_ref, v_ref, kb_ref, vb_ref,
                   qi_ref, ga_ref, gb_ref, ki_ref, kib_ref, wi_ref, *key_major_refs):
    n = _rms(x_ref[...], g_ref[...]).astype(BF16)

    def mm(a, b):
        return _dot(n, w_ref[:, a:b])

    rx_ref[...] = mm(_C_RX, _C_RG)
    rg_ref[...] = mm(_C_RG, _C_Q)
    q_ref[...] = (mm(_C_Q, _C_K) * Q_SCALE).astype(BF16)
    k = mm(_C_K, _C_V)
    k_ref[...] = k
    kb_ref[...] = k.astype(BF16)
    v = mm(_C_V, _C_QI)
    v_ref[...] = v
    vb_ref[...] = v.astype(BF16)
    qi_ref[...] = mm(_C_QI, _C_GA).astype(BF16)
    ga_ref[...] = mm(_C_GA, _C_GB)
    gb_ref[...] = mm(_C_GB, _C_TAIL)
    tail = mm(_C_TAIL, _C_END)
    ki = tail[:, :IDX_DIM]
    ki_ref[...] = ki
    kib_ref[...] = ki.astype(BF16)
    wi_ref[...] = tail[:, IDX_DIM:IDX_DIM + IDX_HEADS]
    if key_major_refs:
        kt_ref, kie_ref, kio_ref = key_major_refs
        kt_ref[0] = k.T.astype(BF16)
        tail_t = tail.T
        kit = tail_t[:IDX_DIM].astype(BF16)
        zeros = jnp.zeros_like(kit)
        kie_ref[0] = jnp.concatenate([kit, zeros], axis=0)
        kio_ref[0] = jnp.concatenate([zeros, kit], axis=0)


def _inproj(x2, g_mix, w_r, seq_len):
    n = x2.shape[0]
    kvd = N_KV_HEADS * HEAD_DIM
    tok = lambda w: pl.BlockSpec((TM, w), lambda i: (i, 0))
    extra_specs, extra_shapes = [], []
    if seq_len % TM == 0:
        per_seq = seq_len // TM
        kmaj = lambda r: pl.BlockSpec((1, r, TM), lambda i: (i // per_seq, 0, i % per_seq))
        extra_specs = [kmaj(kvd), kmaj(LANES), kmaj(LANES)]
        extra_shapes = [jax.ShapeDtypeStruct((n // seq_len, r, seq_len), BF16) for r in (kvd, LANES, LANES)]
    widths_dtypes = [(D_RNN, F32), (D_RNN, F32), (D_MODEL, BF16), (kvd, F32), (kvd, F32), (kvd, BF16),
                     (kvd, BF16), (IDX_HEADS * IDX_DIM, BF16), (D_MODEL, F32), (D_MODEL, F32),
                     (IDX_DIM, F32), (IDX_DIM, BF16), (IDX_HEADS, F32)]
    return pl.pallas_call(
        _inproj_kernel,
        grid=(n // TM,),
        in_specs=[tok(D_MODEL), _const_spec((1, D_MODEL)), _const_spec(w_r.shape)],
        out_specs=[tok(w) for w, _ in widths_dtypes] + extra_specs,
        out_shape=[jax.ShapeDtypeStruct((n, w), dt) for w, dt in widths_dtypes] + extra_shapes,
        compiler_params=_params(("parallel",)),
        name="inproj",
    )(x2, g_mix, w_r)


def _rglru_kernel(x_ref, gate_ref, cs_ref, h0_ref, cw_ref, cb_ref, wa_ref, ba_ref, wx_ref, bx_ref,
                  lam_ref, hg_ref, cnew_ref, hlast_ref, xp_s, hc_s, *, tt):
    @pl.when(pl.program_id(1) == 0)
    def _():
        xp_s[5:8, :] = cs_ref[0]
        hc_s[...] = h0_ref[0]

    xp_s[8:8 + tt, :] = x_ref[0]
    nl = -lam_ref[...]
    softplus = jnp.maximum(nl, 0.0) + jnp.log1p(jnp.exp(-jnp.abs(nl)))
    row = lax.broadcasted_iota(I32, (tt, RG_BLOCK), 0)
    for n in range(RG_BLOCKS):
        sl = slice(n * RG_BLOCK, (n + 1) * RG_BLOCK)
        xc = cb_ref[:, sl]
        for j in range(CONV_W):
            xc = xc + cw_ref[j:j + 1, sl] * xp_s[5 + j:5 + j + tt, sl]
        xcb = xc.astype(BF16)
        r = _sigmoid(_dot(xcb, wa_ref[n]) + ba_ref[:, sl])
        ig = _sigmoid(_dot(xcb, wx_ref[n]) + bx_ref[:, sl])
        log_a = -RG_C * r * softplus[:, sl]
        a = jnp.exp(log_a)
        mult = jnp.sqrt(-jnp.tanh(log_a) * (a * a + 1.0))
        bv = mult * (ig * xc)
        d = 1
        while d < tt:
            keep = row >= d
            a_sh = pltpu.roll(a, d, 0)
            b_sh = pltpu.roll(bv, d, 0)
            bv = jnp.where(keep, a * b_sh + bv, bv)
            a = jnp.where(keep, a * a_sh, a)
            d *= 2
        h = a * hc_s[:, sl] + bv
        hc_s[:, sl] = h[tt - 1:tt, :]
        hg_ref[0, :, sl] = (h * _gelu(gate_ref[0, :, sl])).astype(BF16)
    tail = xp_s[tt + 5:tt + 8, :]
    cnew_ref[0] = tail
    xp_s[5:8, :] = tail
    hlast_ref[0] = hc_s[...]


def _rglru(rx, rgate, conv_state, h0, conv_w, conv_b, w_a, b_a, w_x, b_x, lam):
    b, t, _ = rx.shape
    tt = min(t, 256)
    seq = pl.BlockSpec((1, tt, D_RNN), lambda i, j: (i, j, 0))
    per_b = lambda r: pl.BlockSpec((1, r, D_RNN), lambda i, j: (i, 0, 0))
    vec = _const_spec((1, D_RNN))
    wblk = _const_spec((RG_BLOCKS, RG_BLOCK, RG_BLOCK))
    return pl.pallas_call(
        functools.partial(_rglru_kernel, tt=tt),
        grid=(b, t // tt),
        in_specs=[seq, seq, per_b(CONV_W - 1), per_b(1), _const_spec((CONV_W, D_RNN)), vec,
                  wblk, vec, wblk, vec, vec],
        out_specs=[seq, per_b(CONV_W - 1), per_b(1)],
        out_shape=[jax.ShapeDtypeStruct((b, t, D_RNN), BF16),
                   jax.ShapeDtypeStruct((b, CONV_W - 1, D_RNN), F32),
                   jax.ShapeDtypeStruct((b, 1, D_RNN), F32)],
        scratch_shapes=[pltpu.VMEM((tt + 8, D_RNN), F32), pltpu.VMEM((1, D_RNN), F32)],
        compiler_params=_params(("parallel", "arbitrary")),
        name="rglru",
    )(rx, rgate, conv_state, h0, conv_w, conv_b, w_a, b_a, w_x, b_x, lam)


def _dsa_kernel(q_ref, qi_ref, wi_ref, kt_ref, v_ref, kie_ref, kio_ref, nb_ref, o_ref,
                key_s, wb_s, qs_s, m_s, l_s, acc_s, *, tq, past, l_real, n_sel):
    kb = DSA_BLK
    nc = kb // LANES
    i = pl.program_id(1)
    q0 = past + i * tq
    own = q0 // kb
    nkb = own + 1
    n_unit = (nkb + 1) // 2

    def lanes_at(k0):
        return lax.shift_right_logical(k0, LANE_SHIFT)

    wi = wi_ref[0] * (IDX_HEADS ** -0.5 * IDX_DIM ** -0.5)
    for h in range(IDX_HEADS):
        wb_s[h] = jnp.broadcast_to(wi[:, h:h + 1], (tq, LANES))
    q_chunk = (q0 + lax.broadcasted_iota(I32, (tq, LANES), 0)) >> CHUNK_SHIFT
    lane = lax.broadcasted_iota(I32, (tq, LANES), 1)

    def score_block(j, carry):
        k0 = pl.multiple_of(j * kb, kb)
        acc = [jnp.zeros((tq, LANES), F32) for _ in range(nc)]
        for h2 in range(IDX_HEADS // 2):
            qpair = qi_ref[0, :, h2 * LANES:(h2 + 1) * LANES]
            for par, kref in ((0, kie_ref), (1, kio_ref)):
                s = _dot(qpair, kref[0, :, pl.ds(k0, kb)])
                w = wb_s[2 * h2 + par]
                for c in range(nc):
                    acc[c] = acc[c] + jnp.maximum(s[:, c * LANES:(c + 1) * LANES], 0.0) * w
        for c in range(nc):
            kpos = k0 + c * LANES + lane
            key = acc[c] + 0.0
            key = jnp.where((kpos >> CHUNK_SHIFT) <= q_chunk, key, -jnp.inf)
            key = jnp.where(kpos < l_real, key, -jnp.inf)
            key_s[lanes_at(k0 + c * LANES)] = key
        return carry

    lax.fori_loop(0, nkb, score_block, 0)

    @pl.when(nkb % 2 == 1)
    def _():
        for c in range(nc):
            key_s[lanes_at(nkb * kb + c * LANES)] = jnp.full((tq, LANES), -jnp.inf, F32)

    unit = 2 * kb
    strip = min(tq, SEARCH_ROWS)
    n_strip = tq // strip

    def as_float(code):
        return pltpu.bitcast(code ^ ((code >> 31) & 0x7FFFFFFF), F32)

    def search_strip(s):
        rows = slice(s * strip, (s + 1) * strip)

        def count_ge(code):
            cand = as_float(code)

            def body(u, cnt):
                for c in range(unit // LANES):
                    blk = key_s[lanes_at(u * unit + c * LANES), rows, :]
                    cnt = cnt + jnp.where(blk >= cand, 1.0, 0.0)
                return cnt

            cnt = lax.fori_loop(0, n_unit, body, jnp.zeros((strip, LANES), F32))
            return jnp.broadcast_to(jnp.sum(cnt, axis=1, keepdims=True), (strip, LANES))

        n_adm = jnp.minimum((q_chunk[rows] + 1) * CHUNK, l_real).astype(F32)

        def step(carry):
            it, res, cnt, _ = carry
            cand = jnp.where(it == 0, 0, res | (1 << (31 - it)))
            c = count_ge(cand)
            take = c >= n_sel
            res = jnp.where(take, cand, res)
            cnt = jnp.where(take, c, cnt)
            return it + 1, res, cnt, jnp.max(cnt) > n_sel

        init = (jnp.int32(0), jnp.full((strip, LANES), INT_MIN, I32), n_adm, jnp.max(n_adm) > n_sel)
        return lax.while_loop(lambda c: (c[0] < 32) & c[3], step, init)[1]

    res = jnp.concatenate([search_strip(s) for s in range(n_strip)], axis=0)
    thr = jnp.where(res == INT_MIN, -jnp.finfo(F32).max, as_float(res))

    for g in range(N_KV_HEADS):
        for hh in range(KV_GROUP):
            h = g * KV_GROUP + hh
            qs_s[g, hh * tq:(hh + 1) * tq, :] = q_ref[0, :, h * HEAD_DIM:(h + 1) * HEAD_DIM]
    m_s[...] = jnp.full(m_s.shape, NEG, F32)
    l_s[...] = jnp.zeros(l_s.shape, F32)
    acc_s[...] = jnp.zeros(acc_s.shape, F32)

    def attend(k0, width, near):
        ncw = width // LANES
        mb = [jnp.where(key_s[lanes_at(k0 + c * LANES)] >= thr, 0.0, NEG) for c in range(ncw)]
        for g in range(N_KV_HEADS):
            s_all = _dot(qs_s[g], kt_ref[0, g * HEAD_DIM:(g + 1) * HEAD_DIM, pl.ds(k0, width)])
            vb = v_ref[0, pl.ds(k0, width), g * HEAD_DIM:(g + 1) * HEAD_DIM]
            for hh in range(KV_GROUP):
                h = g * KV_GROUP + hh
                s = s_all[hh * tq:(hh + 1) * tq, :]
                sc = []
                for c in range(ncw):
                    x = s[:, c * LANES:(c + 1) * LANES] + mb[c]
                    if near is not None:
                        x = x + nb_ref[near, h, :, c * LANES:(c + 1) * LANES]
                    sc.append(x)
                mx = sc[0]
                for c in range(1, ncw):
                    mx = jnp.maximum(mx, sc[c])
                m_old = m_s[h]
                m_new = jnp.maximum(m_old, jnp.broadcast_to(jnp.max(mx, axis=1, keepdims=True), (tq, LANES)))
                alpha = jnp.exp2(m_old - m_new)
                p = [jnp.exp2(x - m_new) for x in sc]
                ps = p[0]
                for c in range(1, ncw):
                    ps = ps + p[c]
                l_s[h] = alpha * l_s[h] + jnp.broadcast_to(jnp.sum(ps, axis=1, keepdims=True), (tq, LANES))
                m_s[h] = m_new
                pb = jnp.concatenate(p, axis=1).astype(BF16)
                acc_s[h] = alpha * acc_s[h] + _dot(pb, vb)

    n_far = jnp.maximum(own - 1, 0)
    per_wide = DSA_WIDE // kb
    n_wide = n_far // per_wide

    def wide_step(j, carry):
        attend(pl.multiple_of(j * DSA_WIDE, DSA_WIDE), DSA_WIDE, None)
        return carry

    def far_block(j, carry):
        attend(pl.multiple_of(j * kb, kb), kb, None)
        return carry

    lax.fori_loop(0, n_wide, wide_step, 0)
    lax.fori_loop(n_wide * per_wide, n_far, far_block, 0)

    @pl.when(own >= 1)
    def _():
        attend(pl.multiple_of((own - 1) * kb, kb), kb, 0)

    attend(pl.multiple_of(own * kb, kb), kb, 1)

    for h in range(N_HEADS):
        o_ref[0, :, h * HEAD_DIM:(h + 1) * HEAD_DIM] = (acc_s[h] / l_s[h]).astype(BF16)


def _t5_bucket(rel):
    nb = NUM_BUCKETS // 2
    max_exact = nb // 2
    ret = jnp.where(rel > 0, nb, 0)
    n = jnp.abs(rel)
    n_f = jnp.maximum(n, 1).astype(F32)
    large = max_exact + (jnp.log(n_f / max_exact) / math.log(MAX_DISTANCE / max_exact)
                         * (nb - max_exact)).astype(I32)
    large = jnp.minimum(large, nb - 1)
    return ret + jnp.where(n < max_exact, n, large)


def _near_bias(rel_bias, tq):
    a = jnp.arange(tq, dtype=I32)[:, None]
    b = jnp.arange(DSA_BLK, dtype=I32)[None, :]
    centered = (rel_bias - rel_bias[NUM_BUCKETS // 2 - 1]) * LOG2E
    tiles = []
    for d in (-DSA_BLK, 0):
        bucket = jnp.where(d + b - a > -MAX_DISTANCE, _t5_bucket(d + b - a), NUM_BUCKETS // 2 - 1)
        t = jnp.zeros((N_HEADS, tq, DSA_BLK), F32)
        for c in range(NUM_BUCKETS):
            t = jnp.where((bucket == c)[None], centered[c][:, None, None], t)
        tiles.append(t)
    return jnp.stack(tiles)


def _dsa(q, qi, wi, kt, v, kie, kio, nbias, *, past, l_real, n_sel):
    b, t, _ = q.shape
    tq = min(t, DSA_BLK)
    lpad = kt.shape[2]
    tile = lambda w: pl.BlockSpec((1, tq, w), lambda i, j: (i, j, 0))
    per_b = lambda r, c: pl.BlockSpec((1, r, c), lambda i, j: (i, 0, 0), pipeline_mode=pl.Buffered(1))
    kvd = N_KV_HEADS * HEAD_DIM
    return pl.pallas_call(
        functools.partial(_dsa_kernel, tq=tq, past=past, l_real=l_real, n_sel=n_sel),
        grid=(b, t // tq),
        in_specs=[tile(D_MODEL), tile(IDX_HEADS * IDX_DIM), tile(IDX_HEADS),
                  per_b(kvd, lpad), per_b(lpad, kvd), per_b(LANES, lpad), per_b(LANES, lpad),
                  _const_spec(nbias.shape)],
        out_specs=tile(D_MODEL),
        out_shape=jax.ShapeDtypeStruct((b, t, D_MODEL), BF16),
        scratch_shapes=[pltpu.VMEM((lpad // LANES, tq, LANES), F32),
                        pltpu.VMEM((IDX_HEADS, tq, LANES), F32),
                        pltpu.VMEM((N_KV_HEADS, KV_GROUP * tq, HEAD_DIM), BF16),
                        pltpu.VMEM((N_HEADS, tq, LANES), F32),
                        pltpu.VMEM((N_HEADS, tq, LANES), F32),
                        pltpu.VMEM((N_HEADS, tq, HEAD_DIM), F32)],
        compiler_params=_params(("parallel", "arbitrary")),
        name="dsa",
    )(q, qi, wi, kt, v, kie, kio, nbias)


def _merge_kernel(x_ref, hg_ref, at_ref, ga_ref, gb_ref, wa_ref, wb_ref, wo_ref, gf_ref, wq_ref, sk_ref,
                  x1_ref, xnt_ref, st_ref):
    ya = _dot(hg_ref[...], wa_ref[...])
    yb = _dot(at_ref[...], wb_ref[...])
    m = _sigmoid(ga_ref[...]) * ya + _sigmoid(gb_ref[...]) * yb
    x1 = x_ref[...] + _dot(m.astype(BF16), wo_ref[...])
    x1_ref[...] = x1
    xn = _rms(x1, gf_ref[...])
    xnt_ref[...] = xn.T.astype(BF16)
    qp = _dot(xn.astype(BF16), wq_ref[...])
    for j in range(2 * PEER_HEADS):
        qj = qp[:, j * LANES:(j + 1) * LANES].astype(BF16)
        st_ref[j] = _dot_nt(sk_ref[j], qj)


def _merge(x2, hg, attn, ga, gb, w_a_out, w_b_out, w_o, g_ffn, w_q, sk):
    n = x2.shape[0]
    tok = lambda w: pl.BlockSpec((TM, w), lambda i: (i, 0))
    sq = _const_spec((D_MODEL, D_MODEL))
    nsk = 2 * PEER_HEADS
    return pl.pallas_call(
        _merge_kernel,
        grid=(n // TM,),
        in_specs=[tok(D_MODEL)] * 5 + [sq, sq, sq, _const_spec((1, D_MODEL)),
                                       _const_spec(w_q.shape), _const_spec(sk.shape)],
        out_specs=[tok(D_MODEL), pl.BlockSpec((D_MODEL, TM), lambda i: (0, i)),
                   pl.BlockSpec((nsk, PEER_NKEYS, TM), lambda i: (0, 0, i))],
        out_shape=[jax.ShapeDtypeStruct((n, D_MODEL), F32), jax.ShapeDtypeStruct((D_MODEL, n), BF16),
                   jax.ShapeDtypeStruct((nsk, PEER_NKEYS, n), F32)],
        compiler_params=_params(("parallel",)),
        name="merge",
    )(x2, hg, attn, ga, gb, w_a_out, w_b_out, w_o, g_ffn, w_q, sk)


_PAIR_LIMIT = [PEER_TOPK // (i + 1) for i in range(PEER_TOPK)]
_NOT_TOP = 99.0


def _peer_select_kernel(st_ref, r2_ref, e2_ref, n1_ref, e1_ref):
    ninf = -jnp.inf
    sub = 8
    row8 = lax.broadcasted_iota(I32, (sub, PEER_TM), 0)

    def top(s, want_rank):
        vals, cur = [], s
        rank = jnp.full(s.shape, _NOT_TOP, F32) if want_rank else None
        for k in range(PEER_TOPK):
            m = jnp.max(cur, axis=0, keepdims=True)
            vals.append(m)
            sel = cur == m
            if want_rank:
                rank = jnp.where(sel, float(k), rank)
            cur = jnp.where(sel, ninf, cur)
        return vals, rank

    def head(h, carry):
        s1 = st_ref[2 * h]
        s2 = st_ref[2 * h + 1]
        v1, _ = top(s1, False)
        v2, r2 = top(s2, True)
        v2t = jnp.concatenate(v2, axis=0)
        v1_low = jnp.concatenate(v1[sub:], axis=0)
        groups = [v1[0] + v2t]
        for i in range(1, sub):
            groups.append(jnp.where(row8 < _PAIR_LIMIT[i], v1[i] + v2t[:sub], ninf))
        groups.append(v1_low + v2[0])
        cand = jnp.concatenate(groups, axis=0)
        cur = cand
        for r in range(PEER_TOPK):
            thr = jnp.max(cur, axis=0, keepdims=True)
            if r + 1 < PEER_TOPK:
                cur = jnp.where(cur == thr, ninf, cur)
        picked = cand >= thr
        z = jnp.sum(jnp.where(picked, jnp.exp(cand - (v1[0] + v2[0])), 0.0), axis=0, keepdims=True)
        cnt = jnp.where(picked, 1.0, 0.0)
        n_of_rank = [jnp.sum(cnt[:PEER_TOPK], axis=0, keepdims=True)]
        for i in range(1, sub):
            lo = PEER_TOPK + (i - 1) * sub
            n_of_rank.append(jnp.sum(cnt[lo:lo + sub], axis=0, keepdims=True))
        lo = PEER_TOPK + (sub - 1) * sub
        for i in range(sub, PEER_TOPK):
            n_of_rank.append(cnt[lo + i - sub:lo + i - sub + 1])
        n1 = jnp.zeros(s1.shape, F32)
        for i in range(PEER_TOPK):
            n1 = jnp.where(s1 == v1[i], n_of_rank[i], n1)
        r2_ref[h] = r2.astype(BF16)
        e2_ref[h] = (jnp.exp(s2 - v2[0]) / z).astype(BF16)
        n1_ref[h] = n1
        e1_ref[h] = jnp.exp(s1 - v1[0])
        return carry

    lax.fori_loop(0, PEER_HEADS, head, 0)


def _peer_select(st):
    n = st.shape[2]
    big = pl.BlockSpec((PEER_HEADS, PEER_NKEYS, PEER_TM), lambda i: (0, 0, i))
    sds = lambda dt: jax.ShapeDtypeStruct((PEER_HEADS, PEER_NKEYS, n), dt)
    return pl.pallas_call(
        _peer_select_kernel,
        grid=(n // PEER_TM,),
        in_specs=[pl.BlockSpec((2 * PEER_HEADS, PEER_NKEYS, PEER_TM), lambda i: (0, 0, i))],
        out_specs=[big, big, big, big],
        out_shape=[sds(BF16), sds(BF16), sds(F32), sds(F32)],
        compiler_params=_params(("parallel",)),
        name="peer_select",
    )(st)


def _gelu_folded(x):
    k = 2.0 * math.sqrt(2.0 / math.pi) * LOG2E
    return x / (1.0 + jnp.exp2(x * (-k - (k * 0.044715) * (x * x))))


def _peer_dense_kernel(xnt_ref, u_ref, vt_ref, r2_ref, e2_ref, n1_ref, e1_ref, o_ref, acc_s):
    e = pl.program_id(1)
    pack = 16
    n_pack = PEER_NKEYS // pack

    @pl.when(e == 0)
    def _():
        acc_s[...] = jnp.zeros(acc_s.shape, F32)

    def gate_weights(a):
        w = [jnp.zeros((pack, PEER_TM), BF16) for _ in range(n_pack)]
        for h in range(PEER_HEADS):
            n1 = jnp.broadcast_to(n1_ref[h, pl.ds(a, 1), :], (pack, PEER_TM)).astype(BF16)
            e1 = jnp.broadcast_to(e1_ref[h, pl.ds(a, 1), :], (pack, PEER_TM)).astype(BF16)
            for r in range(n_pack):
                rows = slice(r * pack, (r + 1) * pack)
                w[r] = w[r] + jnp.where(r2_ref[h, rows, :] < n1, e2_ref[h, rows, :], 0.0) * e1
        return jnp.concatenate(w, axis=0)

    xnt = xnt_ref[...]
    g = []
    row0 = 0
    for rows_k in PEER_CHUNK_ROWS:
        n_sub = rows_k // PEER_NKEYS
        a0 = (e * PEER_STEP + row0) // PEER_NKEYS
        w = [gate_weights(a0 + aa) for aa in range(n_sub)]
        act_t = _dot(u_ref[row0:row0 + rows_k, :], xnt)
        g += [w[aa] * _gelu_folded(act_t[aa * PEER_NKEYS:(aa + 1) * PEER_NKEYS, :]).astype(BF16)
              for aa in range(n_sub)]
        row0 += rows_k
    acc_s[...] += _dot(vt_ref[...], jnp.concatenate(g, axis=0))

    @pl.when(e == pl.num_programs(1) - 1)
    def _():
        o_ref[...] = acc_s[...].T


def _peer_dense(xnt, u_b, vt_b, r2, e2, n1, e1):
    n = xnt.shape[1]
    step = PEER_STEP
    big = pl.BlockSpec((PEER_HEADS, PEER_NKEYS, PEER_TM), lambda i, e: (0, 0, i))
    return pl.pallas_call(
        _peer_dense_kernel,
        grid=(n // PEER_TM, PEER_EXPERTS // step),
        in_specs=[pl.BlockSpec((D_MODEL, PEER_TM), lambda i, e: (0, i)),
                  pl.BlockSpec((step, D_MODEL), lambda i, e: (e, 0)),
                  pl.BlockSpec((D_MODEL, step), lambda i, e: (0, e)),
                  big, big, big, big],
        out_specs=pl.BlockSpec((PEER_TM, D_MODEL), lambda i, e: (i, 0)),
        out_shape=jax.ShapeDtypeStruct((n, D_MODEL), F32),
        scratch_shapes=[pltpu.VMEM((D_MODEL, PEER_TM), F32)],
        compiler_params=_params(("parallel", "arbitrary")),
        name="peer_dense",
    )(xnt, u_b, vt_b, r2, e2, n1, e1)


def _final_kernel(x1_ref, po_ref, p_ref, gp_ref, wg_ref, wp_ref, gfin_ref, y_ref):
    x2 = x1_ref[...] + po_ref[...]
    gate = _sigmoid(_dot(_rms(x2, gp_ref[...]).astype(BF16), wg_ref[...]))
    x3 = x2 + gate * _dot(p_ref[...].astype(BF16), wp_ref[...])
    y_ref[...] = _rms(x3, gfin_ref[...])


def _final(x1, po, p2, g_ple, w_gate, w_proj, g_final):
    n = x1.shape[0]
    tok = lambda w: pl.BlockSpec((TM, w), lambda i: (i, 0))
    vec = _const_spec((1, D_MODEL))
    return pl.pallas_call(
        _final_kernel,
        grid=(n // TM,),
        in_specs=[tok(D_MODEL), tok(D_MODEL), tok(D_PLE), vec, _const_spec((D_MODEL, D_MODEL)),
                  _const_spec((D_PLE, D_MODEL)), vec],
        out_specs=tok(D_MODEL),
        out_shape=jax.ShapeDtypeStruct((n, D_MODEL), F32),
        compiler_params=_params(("parallel",)),
        name="final",
    )(x1, po, p2, g_ple, w_gate, w_proj, g_final)


def _pad_keys(x, lpad, axis):
    pad = lpad - x.shape[axis]
    if pad == 0:
        return x
    widths = [(0, 0)] * x.ndim
    widths[axis] = (0, pad)
    return jnp.pad(x, widths)


def _layer(x, p, conv_state, h_state, cache_k, cache_v, cache_ki, w):
    b, t, _ = x.shape
    n = b * t
    kvd = N_KV_HEADS * HEAD_DIM
    (rx, rgate, q, k, v, kb, vb, qi, ga, gb, ki, kib, wi, *key_major) = _inproj(
        x.reshape(n, D_MODEL), w["g_mix"], w["w_in"], t)

    hg, conv_new, h_last = _rglru(rx.reshape(b, t, D_RNN), rgate.reshape(b, t, D_RNN), conv_state,
                                  h_state.reshape(b, 1, D_RNN), w["conv_w"], w["conv_b"], w["w_rg_a"],
                                  w["b_rg_a"], w["w_rg_x"], w["b_rg_x"], w["rg_lambda"])

    k_all, v_all, ki_all = kb.reshape(b, t, kvd), vb.reshape(b, t, kvd), kib.reshape(b, t, IDX_DIM)
    past = 0
    if cache_k is not None:
        past = cache_k.shape[1]
        k_all = jnp.concatenate([cache_k.reshape(b, past, kvd).astype(BF16), k_all], axis=1)
        v_all = jnp.concatenate([cache_v.reshape(b, past, kvd).astype(BF16), v_all], axis=1)
        ki_all = jnp.concatenate([cache_ki.astype(BF16), ki_all], axis=1)
    l_real = past + t
    assert past % DSA_BLK == 0 and t % min(t, DSA_BLK) == 0 and n % PEER_TM == 0 and n % TM == 0
    lpad = -(-l_real // (2 * DSA_BLK)) * (2 * DSA_BLK)
    if past == 0 and key_major and lpad == t:
        kt, kie, kio = key_major
    else:
        kt = _pad_keys(jnp.swapaxes(k_all, 1, 2), lpad, 2)
        kit = _pad_keys(jnp.swapaxes(ki_all, 1, 2), lpad, 2)
        zeros = jnp.zeros_like(kit)
        kie = jnp.concatenate([kit, zeros], axis=1)
        kio = jnp.concatenate([zeros, kit], axis=1)
    v_all = _pad_keys(v_all, lpad, 1)
    tq = min(t, DSA_BLK)
    attn = _dsa(q.reshape(b, t, D_MODEL), qi.reshape(b, t, IDX_HEADS * IDX_DIM), wi.reshape(b, t, IDX_HEADS),
                kt, v_all, kie, kio, _near_bias(w["rel_bias"], tq),
                past=past, l_real=l_real, n_sel=min(TOPK_MAX, l_real // 4))
    x1, xnt, st = _merge(x.reshape(n, D_MODEL), hg.reshape(n, D_RNN), attn.reshape(n, D_MODEL), ga, gb,
                        w["w_a_out"], w["w_b_out"], w["w_o"], w["g_ffn"], w["w_peer_q"], w["peer_sk"])
    r2, e2, n1, e1 = _peer_select(st)
    po = _peer_dense(xnt, w["peer_u"], w["peer_vt"], r2, e2, n1, e1)
    y = _final(x1, po, p.reshape(n, D_PLE), w["g_ple"], w["w_ple_gate"], w["w_ple_proj"], w["g_final"])
    return (y.reshape(b, t, D_MODEL), k.reshape(b, t, N_KV_HEADS, HEAD_DIM),
            v.reshape(b, t, N_KV_HEADS, HEAD_DIM), ki.reshape(b, t, IDX_DIM), conv_new,
            h_last.reshape(b, D_RNN))


def kernel(x_prompt, x_sample, p_prompt, p_sample, state_conv, state_rglru, cache_k, cache_v, cache_idx_k, rel_bias, g_mix, w_in, conv_w, conv_b, w_rg_a, b_rg_a, w_rg_x, b_rg_x, rg_lambda, w_a_out, w_b_out, w_o, g_ffn, w_peer_q, peer_sub_keys, peer_u, peer_v, g_ple, w_ple_gate, w_ple_proj, g_final):
    assert g_mix.shape[0] == 1, "single trunk layer"
    row = lambda a: a.reshape(1, -1)
    wi_full = w_in[0]
    cut = _C_GA
    tail = IDX_DIM + IDX_HEADS
    w_r = jnp.concatenate([wi_full[:, :cut], wi_full[:, cut + tail:], wi_full[:, cut:cut + tail],
                           jnp.zeros((D_MODEL, _C_END - _C_TAIL - tail), F32)], axis=1).astype(BF16)
    w = dict(
        g_mix=row(g_mix[0]), w_in=w_r, conv_w=conv_w[0], conv_b=row(conv_b[0]),
        w_rg_a=w_rg_a[0].astype(BF16), b_rg_a=row(b_rg_a[0]), w_rg_x=w_rg_x[0].astype(BF16),
        b_rg_x=row(b_rg_x[0]), rg_lambda=row(rg_lambda[0]), rel_bias=rel_bias,
        w_a_out=w_a_out[0].astype(BF16), w_b_out=w_b_out[0].astype(BF16), w_o=w_o[0].astype(BF16),
        g_ffn=row(g_ffn[0]), w_peer_q=w_peer_q[0].astype(BF16),
        peer_sk=peer_sub_keys[0].reshape(2 * PEER_HEADS, PEER_NKEYS, PEER_DK // 2).astype(BF16),
        peer_u=peer_u[0].astype(BF16), peer_vt=peer_v[0].astype(BF16).T,
        g_ple=row(g_ple[0]), w_ple_gate=w_ple_gate[0].astype(BF16), w_ple_proj=w_ple_proj[0].astype(BF16),
        g_final=row(g_final),
    )
    bp = x_prompt.shape[0]
    zc = jnp.zeros((bp, CONV_W - 1, D_RNN), F32)
    zh = jnp.zeros((bp, D_RNN), F32)
    yp, k1, v1, ki1, c1, r1 = _layer(x_prompt, p_prompt[0], zc, zh, None, None, None, w)
    ys, k2, v2, ki2, c2, r2 = _layer(x_sample, p_sample[0], state_conv[0], state_rglru[0],
                                     cache_k[0], cache_v[0], cache_idx_k[0], w)
    return (yp, ys, k1[None], v1[None], ki1[None], c1[None], r1[None],
            k2[None], v2[None], ki2[None], c2[None], r2[None])
```
